```python
import jax
import jax.numpy as jnp
from jax import lax
import numpy as np

D_MODEL = 1024
BATCH = 2
SEQ = 16384
DEPTH = 2

GRID_W = 64
N_MIXERS = 2
NORM_EPS = 1e-6
ROPE_THETA = 10000.0
RET_HEADS = 4
RET_DK = 256
RET_DV = 512
RET_QK = RET_HEADS * RET_DK
RET_V = RET_HEADS * RET_DV
RET_CHUNK = 128
ATT_HEADS = 8
ATT_KV_HEADS = 2
ATT_HD = 128
ATT_GROUP = ATT_HEADS // ATT_KV_HEADS
ATT_Q = ATT_HEADS * ATT_HD
ATT_KV = ATT_KV_HEADS * ATT_HD
Q_BLOCK = 128
D_FF = 2816
N_EXPERTS = 8
TOP_K = 2
D_FF_EXPERT = 3584

kernel_name = 'hybrid_retention_gqa_moe_adaln_encoder'


def rmsnorm(x, g):
    x32 = x.astype(jnp.float32)
    y = x32 * lax.rsqrt(jnp.mean(x32 * x32, axis=-1, keepdims=True) + NORM_EPS)
    return y.astype(x.dtype) * g


def modulate(h, shift, scale):
    return h * (1 + scale[:, None, :]) + shift[:, None, :]


def rope_1d(x, pos):
    d = x.shape[-1]
    freqs = ROPE_THETA ** (-jnp.arange(0, d // 2, dtype=jnp.float32) * 2.0 / d)
    ang = pos[:, None] * freqs[None, :]
    cos = jnp.cos(ang)[None, :, None, :].astype(x.dtype)
    sin = jnp.sin(ang)[None, :, None, :].astype(x.dtype)
    x1, x2 = jnp.split(x, 2, axis=-1)
    return jnp.concatenate([x1 * cos - x2 * sin, x1 * sin + x2 * cos], axis=-1)


def axial_rope(x, row, col):
    half = x.shape[-1] // 2
    return jnp.concatenate([rope_1d(x[..., :half], row), rope_1d(x[..., half:], col)], axis=-1)


def grid_positions(seq_len):
    rows = seq_len // GRID_W
    row = jnp.repeat(jnp.arange(rows, dtype=jnp.float32), GRID_W)
    col = jnp.tile(jnp.arange(GRID_W, dtype=jnp.float32), rows)
    return row, col


def chunk_retention(q, k, v, gamma, include_diag):
    bsz, seq, heads, dk = q.shape
    dv = v.shape[-1]
    n = seq // RET_CHUNK
    log_g = jnp.log(gamma)
    idx = jnp.arange(RET_CHUNK, dtype=jnp.float32)
    diff = idx[:, None] - idx[None, :]
    mask = diff >= 0 if include_diag else diff > 0
    d_intra = jnp.where(mask[None], jnp.exp(jnp.maximum(diff, 0.0)[None] * log_g[:, None, None]), 0.0).astype(q.dtype)
    xi = jnp.exp((idx[:, None] + 1.0) * log_g[None, :]).astype(q.dtype)
    zeta = jnp.exp((RET_CHUNK - 1.0 - idx[:, None]) * log_g[None, :]).astype(q.dtype)
    chunk_decay = jnp.exp(RET_CHUNK * log_g).astype(q.dtype)
    qc = q.reshape(bsz, n, RET_CHUNK, heads, dk)
    kc = k.reshape(bsz, n, RET_CHUNK, heads, dk)
    vc = v.reshape(bsz, n, RET_CHUNK, heads, dv)
    scores = jnp.einsum('bnihd,bnjhd->bnhij', qc, kc) * d_intra[None, None]
    y_intra = jnp.einsum('bnhij,bnjhe->bnihe', scores, vc)

    def step(state, inp):
        q_i, k_i, v_i = inp
        y_cross = jnp.einsum('bihd,bhde->bihe', q_i, state) * xi[None, :, :, None]
        state = state * chunk_decay[None, :, None, None] + jnp.einsum('bjhd,bjhe->bhde', k_i * zeta[None, :, :, None], v_i)
        return state, y_cross

    xs = (jnp.moveaxis(qc, 1, 0), jnp.moveaxis(kc, 1, 0), jnp.moveaxis(vc, 1, 0))
    state0 = jnp.zeros((bsz, heads, dk, dv), q.dtype)
    _, y_cross = lax.scan(step, state0, xs)
    y = y_intra + jnp.moveaxis(y_cross, 0, 1)
    return y.reshape(bsz, seq, heads, dv)


def retention_mixer(h, w_in, decay_logit, gn_g, w_out, row, col):
    bsz, seq, _ = h.shape
    proj = h @ w_in
    q, k, v, g = jnp.split(proj, [RET_QK, 2 * RET_QK, 2 * RET_QK + RET_V], axis=-1)
    q = axial_rope(q.reshape(bsz, seq, RET_HEADS, RET_DK), row, col)
    k = axial_rope(k.reshape(bsz, seq, RET_HEADS, RET_DK), row, col) * (RET_DK ** -0.5)
    v = v.reshape(bsz, seq, RET_HEADS, RET_DV)
    gamma = jax.nn.sigmoid(decay_logit.astype(jnp.float32))
    fwd = chunk_retention(q, k, v, gamma[0], True)
    bwd = jnp.flip(chunk_retention(jnp.flip(q, 1), jnp.flip(k, 1), jnp.flip(v, 1), gamma[1], False), 1)
    y = (fwd + bwd).astype(jnp.float32)
    mu = jnp.mean(y, axis=-1, keepdims=True)
    var = jnp.mean(jnp.square(y - mu), axis=-1, keepdims=True)
    y = ((y - mu) * lax.rsqrt(var + NORM_EPS)).reshape(bsz, seq, RET_V).astype(h.dtype) * gn_g
    return (jax.nn.silu(g) * y) @ w_out


def gqa_mixer(h, w_qkv, q_norm_g, k_norm_g, w_out, row, col):
    bsz, seq, _ = h.shape
    proj = h @ w_qkv
    q, k, v = jnp.split(proj, [ATT_Q, ATT_Q + ATT_KV], axis=-1)
    q = axial_rope(rmsnorm(q.reshape(bsz, seq, ATT_HEADS, ATT_HD), q_norm_g), row, col)
    k = axial_rope(rmsnorm(k.reshape(bsz, seq, ATT_KV_HEADS, ATT_HD), k_norm_g), row, col)
    v = v.reshape(bsz, seq, ATT_KV_HEADS, ATT_HD)
    nb = seq // Q_BLOCK
    qb = q.reshape(bsz, nb, Q_BLOCK, ATT_KV_HEADS, ATT_GROUP, ATT_HD).transpose(1, 0, 2, 3, 4, 5)
    scale = ATT_HD ** -0.5

    def block(q_blk):
        s = jnp.einsum('bqkgd,bskd->bkgqs', q_blk, k).astype(jnp.float32) * scale
        p = jax.nn.softmax(s, axis=-1).astype(v.dtype)
        return jnp.einsum('bkgqs,bskd->bqkgd', p, v)

    o = lax.map(block, qb)
    o = o.transpose(1, 0, 2, 3, 4, 5).reshape(bsz, seq, ATT_Q)
    return o @ w_out


def swiglu(h, w_gate, w_up, w_down):
    return (jax.nn.silu(h @ w_gate) * (h @ w_up)) @ w_down


def moe_swiglu(h, w_router, b_router, w_gate, w_up, w_down):
    logits = (h @ w_router).astype(jnp.float32) + b_router.astype(jnp.float32)
    top_v, top_i = lax.top_k(logits, TOP_K)
    top_w = jax.nn.softmax(top_v, axis=-1)
    combine = jnp.sum(jax.nn.one_hot(top_i, N_EXPERTS, dtype=jnp.float32) * top_w[..., None], axis=-2).astype(h.dtype)
    out = jnp.zeros_like(h)
    for e in range(N_EXPERTS):
        out = out + combine[..., e:e + 1] * swiglu(h, w_gate[e], w_up[e], w_down[e])
    return out


def setup_inputs(seed: int = 0) -> dict:
    key = jax.random.key(seed)
    ks = iter(jax.random.split(key, 40))
    D = D_MODEL

    def nrm(shape, scale):
        return jax.random.normal(next(ks), shape, jnp.float32) * scale

    def gain(n):
        return 1.0 + nrm((n,), 0.02)

    a = 5.0 + jnp.arange(RET_HEADS, dtype=jnp.float32)
    base_logit = jnp.log(2.0 ** a - 1.0)
    inp = {}
    inp['x'] = nrm((BATCH, SEQ, D), 1.0)
    inp['c'] = nrm((BATCH, D), 1.0)
    inp['l0_ada_w'] = nrm((D, 6 * D), 0.5 * D ** -0.5)
    inp['l0_ada_b'] = nrm((6 * D,), 0.01)
    inp['l0_norm1_g'] = gain(D)
    inp['l0_norm2_g'] = gain(D)
    inp['l0_ret_w_in'] = nrm((D, 2 * RET_QK + 2 * RET_V), D ** -0.5)
    inp['l0_ret_decay_logit'] = base_logit[None, :] + nrm((2, RET_HEADS), 0.01)
    inp['l0_ret_gn_g'] = gain(RET_V)
    inp['l0_ret_w_out'] = nrm((RET_V, D), RET_V ** -0.5)
    inp['l0_ffn_w_gate'] = nrm((D, D_FF), D ** -0.5)
    inp['l0_ffn_w_up'] = nrm((D, D_FF), D ** -0.5)
    inp['l0_ffn_w_down'] = nrm((D_FF, D), D_FF ** -0.5)
    inp['l1_ada_w'] = nrm((D, 6 * D), 0.5 * D ** -0.5)
    inp['l1_ada_b'] = nrm((6 * D,), 0.01)
    inp['l1_norm1_g'] = gain(D)
    inp['l1_norm2_g'] = gain(D)
    inp['l1_attn_w_qkv'] = nrm((D, ATT_Q + 2 * ATT_KV), D ** -0.5)
    inp['l1_attn_q_norm_g'] = gain(ATT_HD)
    inp['l1_attn_k_norm_g'] = gain(ATT_HD)
    inp['l1_attn_w_out'] = nrm((ATT_Q, D), ATT_Q ** -0.5)
    inp['l1_moe_w_router'] = nrm((D, N_EXPERTS), D ** -0.5)
    inp['l1_moe_b_router'] = nrm((N_EXPERTS,), 0.01)
    inp['l1_moe_w_gate'] = nrm((N_EXPERTS, D, D_FF_EXPERT), D ** -0.5)
    inp['l1_moe_w_up'] = nrm((N_EXPERTS, D, D_FF_EXPERT), D ** -0.5)
    inp['l1_moe_w_down'] = nrm((N_EXPERTS, D_FF_EXPERT, D), D_FF_EXPERT ** -0.5)
    inp['final_ada_w'] = nrm((D, 2 * D), 0.5 * D ** -0.5)
    inp['final_ada_b'] = nrm((2 * D,), 0.01)
    inp['final_norm_g'] = gain(D)
    return inp


def reference(x, c, l0_ada_w, l0_ada_b, l0_norm1_g, l0_norm2_g, l0_ret_w_in, l0_ret_decay_logit, l0_ret_gn_g, l0_ret_w_out, l0_ffn_w_gate, l0_ffn_w_up, l0_ffn_w_down, l1_ada_w, l1_ada_b, l1_norm1_g, l1_norm2_g, l1_attn_w_qkv, l1_attn_q_norm_g, l1_attn_k_norm_g, l1_attn_w_out, l1_moe_w_router, l1_moe_b_router, l1_moe_w_gate, l1_moe_w_up, l1_moe_w_down, final_ada_w, final_ada_b, final_norm_g):
    seq = x.shape[1]
    row, col = grid_positions(seq)
    c_act = jax.nn.silu(c)
    layers = [
        (l0_ada_w, l0_ada_b, l0_norm1_g, l0_norm2_g,
         (l0_ret_w_in, l0_ret_decay_logit, l0_ret_gn_g, l0_ret_w_out),
         (l0_ffn_w_gate, l0_ffn_w_up, l0_ffn_w_down)),
        (l1_ada_w, l1_ada_b, l1_norm1_g, l1_norm2_g,
         (l1_attn_w_qkv, l1_attn_q_norm_g, l1_attn_k_norm_g, l1_attn_w_out),
         (l1_moe_w_router, l1_moe_b_router, l1_moe_w_gate, l1_moe_w_up, l1_moe_w_down)),
    ]
    for i in range(DEPTH):
        ada_w, ada_b, n1, n2, mix_p, ffn_p = layers[i]
        mod = c_act @ ada_w + ada_b
        sh1, sc1, g1, sh2, sc2, g2 = jnp.split(mod, 6, axis=-1)
        h = modulate(rmsnorm(x, n1), sh1, sc1)
        if i % N_MIXERS == 0:
            m = retention_mixer(h, *mix_p, row, col)
        else:
            m = gqa_mixer(h, *mix_p, row, col)
        x = x + g1[:, None, :] * m
        h = modulate(rmsnorm(x, n2), sh2, sc2)
        if i % 2 == 0:
            f = swiglu(h, *ffn_p)
        else:
            f = moe_swiglu(h, *ffn_p)
        x = x + g2[:, None, :] * f
    fmod = c_act @ final_ada_w + final_ada_b
    fsh, fsc = jnp.split(fmod, 2, axis=-1)
    return modulate(rmsnorm(x, final_norm_g), fsh, fsc)
```

```python
import functools

import numpy as np
import jax
import jax.numpy as jnp
from jax import lax
from jax.experimental import pallas as pl
from jax.experimental.pallas import tpu as pltpu

GRID_W = 64
NORM_EPS = 1e-6
ROPE_THETA = 10000.0
RET_HEADS = 4
RET_DK = 256
RET_DV = 512
RET_CHUNK = 128
ATT_HEADS = 8
ATT_KV_HEADS = 2
ATT_HD = 128
ATT_GROUP = ATT_HEADS // ATT_KV_HEADS
N_EXPERTS = 8
LANES = 128

F32 = jnp.float32
BF16 = jnp.bfloat16
HIGHEST = lax.Precision.HIGHEST

TOKEN_TILE = 512
RET_BLOCK = 512
ATT_Q_TILE = 256
ATT_K_TILE = 512
MOE_TILE = 512
MOE_FF_TILE = 512
COMBINE_TILE = 256
VMEM_LIMIT = 56 * 1024 * 1024


def _params(**kw):
    return pltpu.CompilerParams(vmem_limit_bytes=VMEM_LIMIT, **kw)


def _resident(shape):
    return pl.BlockSpec(shape, lambda *_: (0,) * len(shape), pipeline_mode=pl.Buffered(1))


def _silu(a):
    return a * jax.nn.sigmoid(a)


def _rms_mod(x, g, shift, scale):
    y = x * lax.rsqrt(jnp.mean(x * x, axis=-1, keepdims=True) + NORM_EPS)
    return (y * g) * (1.0 + scale) + shift


def _adaln_kernel(c_ref, w_ref, b_ref, o_ref):
    a = _silu(c_ref[...])
    o_ref[...] = jnp.dot(a, w_ref[...], precision=HIGHEST, preferred_element_type=F32) + b_ref[...]


def _adaln(c8, w, b):
    d, n = w.shape
    tn = 1024
    return pl.pallas_call(
        _adaln_kernel,
        grid=(n // tn,),
        in_specs=[
            pl.BlockSpec((8, d), lambda j: (0, 0)),
            pl.BlockSpec((d, tn), lambda j: (0, j)),
            pl.BlockSpec((1, tn), lambda j: (0, j)),
        ],
        out_specs=pl.BlockSpec((8, tn), lambda j: (0, j)),
        out_shape=jax.ShapeDtypeStruct((8, n), F32),
        compiler_params=_params(),
        name="adaln",
    )(c8, w, b.reshape(1, n))


def _rope_tables(seq, half):
    nfreq = half // 2
    freqs = ROPE_THETA ** (-np.arange(0, nfreq, dtype=np.float64) * 2.0 / half)
    npos = max(seq // GRID_W, GRID_W)
    ang = np.arange(npos, dtype=np.float64)[:, None] * freqs[None, :]
    cos = np.concatenate([np.cos(ang), np.cos(ang)], axis=1)
    sin = np.concatenate([-np.sin(ang), np.sin(ang)], axis=1)
    return cos.astype(np.float32), sin.astype(np.float32)


def _inproj0_kernel(x_ref, g_ref, sh_ref, sc_ref, w_ref, cr_ref, sr_ref, cc_ref, scol_ref, o_ref, *, tm):
    h = _rms_mod(x_ref[...], g_ref[...], sh_ref[...], sc_ref[...]).astype(BF16)
    rows = tm // GRID_W
    cr = cr_ref[...]
    sr = sr_ref[...]
    cc = cc_ref[...][None]
    scol = scol_ref[...][None]
    nw = 512
    n_qk = 2 * RET_HEADS * RET_DK
    for c in range(o_ref.shape[1] // nw):
        y = jnp.dot(h, w_ref[:, c * nw:(c + 1) * nw], preferred_element_type=F32)
        if c * nw < n_qk:
            is_k = c * nw >= RET_HEADS * RET_DK
            for s in range(nw // LANES):
                ys = y[:, s * LANES:(s + 1) * LANES]
                y3 = ys.reshape(rows, GRID_W, LANES)
                r3 = pltpu.roll(ys, LANES // 2, 1).reshape(rows, GRID_W, LANES)
                if s % 2 == 0:
                    o3 = y3 * cr + r3 * sr
                else:
                    o3 = y3 * cc + r3 * scol
                if is_k:
                    o3 = o3 * (RET_DK ** -0.5)
                o_ref[:, c * nw + s * LANES:c * nw + (s + 1) * LANES] = o3.reshape(tm, LANES).astype(BF16)
        else:
            o_ref[:, c * nw:(c + 1) * nw] = y.astype(BF16)


def _inproj0(x, g, sh, sc, w_bf16, seq):
    t, d = x.shape
    n = w_bf16.shape[1]
    tm = TOKEN_TILE
    per_b = seq // tm
    rows = tm // GRID_W
    cos, sin = _rope_tables(seq, LANES)
    n_img_rows = seq // GRID_W
    cr = jnp.asarray(cos[:n_img_rows].reshape(n_img_rows, 1, LANES))
    sr = jnp.asarray(sin[:n_img_rows].reshape(n_img_rows, 1, LANES))
    cc = jnp.asarray(cos[:GRID_W])
    scol = jnp.asarray(sin[:GRID_W])
    vec = lambda: pl.BlockSpec((1, d), lambda i: (0, 0))
    bvec = lambda: pl.BlockSpec((None, 1, d), lambda i: (i // per_b, 0, 0))
    rowt = lambda: pl.BlockSpec((rows, 1, LANES), lambda i: (i % per_b, 0, 0))
    colt = lambda: pl.BlockSpec((GRID_W, LANES), lambda i: (0, 0))
    return pl.pallas_call(
        functools.partial(_inproj0_kernel, tm=tm),
        grid=(t // tm,),
        in_specs=[pl.BlockSpec((tm, d), lambda i: (i, 0)), vec(), bvec(), bvec(),
                  _resident((d, n)), rowt(), rowt(), colt(), colt()],
        out_specs=pl.BlockSpec((tm, n), lambda i: (i, 0)),
        out_shape=jax.ShapeDtypeStruct((t, n), BF16),
        compiler_params=_params(),
        name="l0_inproj",
    )(x, g.reshape(1, d), sh, sc, w_bf16, cr, sr, cc, scol)


def _col(n):
    return lax.broadcasted_iota(jnp.int32, (n, 1), 0).astype(F32)


def _ret_fwd_kernel(lg_ref, q_ref, k_ref, v_ref, y_ref, state_ref, *, nchunks):
    hd = pl.program_id(1)

    @pl.when(pl.program_id(2) == 0)
    def _():
        state_ref[...] = jnp.zeros_like(state_ref)

    c_len = RET_CHUNK
    lgf = lg_ref[0, hd]
    lgb = lg_ref[1, hd]
    ii = lax.broadcasted_iota(jnp.int32, (c_len, c_len), 0)
    jj = lax.broadcasted_iota(jnp.int32, (c_len, c_len), 1)
    diff = (ii - jj).astype(F32)
    dmat = jnp.where(diff >= 0, jnp.exp(diff * lgf), jnp.exp(-diff * lgb))
    idx = _col(c_len)
    xi = jnp.exp((idx + 1.0) * lgf)
    zeta = jnp.exp((c_len - 1.0 - idx) * lgf)
    decay = jnp.exp(jnp.full((1, 1), c_len, F32) * lgf)
    for c in range(nchunks):
        sl = slice(c * c_len, (c + 1) * c_len)
        q = q_ref[sl, :]
        k = k_ref[sl, :]
        v = v_ref[sl, :]
        s = lax.dot_general(q, k, (((1,), (1,)), ((), ())), preferred_element_type=F32) * dmat
        y = jnp.dot(s.astype(BF16), v, preferred_element_type=F32)
        st = state_ref[...]
        y = y + jnp.dot(q, st.astype(BF16), preferred_element_type=F32) * xi
        kz = (k.astype(F32) * zeta).astype(BF16)
        upd = lax.dot_general(kz, v, (((0,), (0,)), ((), ())), preferred_element_type=F32)
        state_ref[...] = st * decay + upd
        y_ref[sl, :] = y.astype(y_ref.dtype)


def _ret_bwd_kernel(lg_ref, q_ref, k_ref, v_ref, g_ref, yf_ref, gn_ref, u_ref, state_ref, *, nchunks):
    hd = pl.program_id(1)

    @pl.when(pl.program_id(2) == 0)
    def _():
        state_ref[...] = jnp.zeros_like(state_ref)

    c_len = RET_CHUNK
    lgb = lg_ref[1, hd]
    idx = _col(c_len)
    xi = jnp.exp((c_len - idx) * lgb)
    zeta = jnp.exp(idx * lgb)
    decay = jnp.exp(jnp.full((1, 1), c_len, F32) * lgb)
    gn = gn_ref[...]
    for c in reversed(range(nchunks)):
        sl = slice(c * c_len, (c + 1) * c_len)
        q = q_ref[sl, :]
        k = k_ref[sl, :]
        v = v_ref[sl, :]
        st = state_ref[...]
        y = yf_ref[sl, :].astype(F32) + jnp.dot(q, st.astype(BF16), preferred_element_type=F32) * xi
        kz = (k.astype(F32) * zeta).astype(BF16)
        upd = lax.dot_general(kz, v, (((0,), (0,)), ((), ())), preferred_element_type=F32)
        state_ref[...] = st * decay + upd
        mu = jnp.mean(y, axis=-1, keepdims=True)
        yc = y - mu
        var = jnp.mean(yc * yc, axis=-1, keepdims=True)
        yn = yc * lax.rsqrt(var + NORM_EPS) * gn
        u_ref[sl, :] = (_silu(g_ref[sl, :].astype(F32)) * yn).astype(u_ref.dtype)


def _retention(proj, log_gamma, gn_g, bsz, seq):
    t = proj.shape[0]
    tb = RET_BLOCK
    nb = seq // tb
    nchunks = tb // RET_CHUNK
    hk = RET_HEADS
    grid = (bsz, RET_HEADS, nb)
    qspec = lambda f: pl.BlockSpec((tb, RET_DK), lambda b, h, n, lg: (b * nb + f(n), h))
    kspec = lambda f: pl.BlockSpec((tb, RET_DK), lambda b, h, n, lg: (b * nb + f(n), hk + h))
    vspec = lambda f: pl.BlockSpec((tb, RET_DV), lambda b, h, n, lg: (b * nb + f(n), hk + h))
    gspec = lambda f: pl.BlockSpec((tb, RET_DV), lambda b, h, n, lg: (b * nb + f(n), 2 * hk + h))
    yspec = lambda f: pl.BlockSpec((tb, RET_DV), lambda b, h, n, lg: (b * nb + f(n), h))
    fwd = lambda n: n
    rev = lambda n: nb - 1 - n
    state = pltpu.VMEM((RET_DK, RET_DV), F32)
    yf = pl.pallas_call(
        functools.partial(_ret_fwd_kernel, nchunks=nchunks),
        grid_spec=pltpu.PrefetchScalarGridSpec(
            num_scalar_prefetch=1, grid=grid,
            in_specs=[qspec(fwd), kspec(fwd), vspec(fwd)],
            out_specs=yspec(fwd), scratch_shapes=[state]),
        out_shape=jax.ShapeDtypeStruct((t, RET_HEADS * RET_DV), BF16),
        compiler_params=_params(),
        name="l0_ret_fwd",
    )(log_gamma, proj, proj, proj)
    return pl.pallas_call(
        functools.partial(_ret_bwd_kernel, nchunks=nchunks),
        grid_spec=pltpu.PrefetchScalarGridSpec(
            num_scalar_prefetch=1, grid=grid,
            in_specs=[qspec(rev), kspec(rev), vspec(rev), gspec(rev), yspec(rev),
                      pl.BlockSpec((1, RET_DV), lambda b, h, n, lg: (0, h))],
            out_specs=yspec(rev), scratch_shapes=[state]),
        out_shape=jax.ShapeDtypeStruct((t, RET_HEADS * RET_DV), BF16),
        compiler_params=_params(),
        name="l0_ret_bwd",
    )(log_gamma, proj, proj, proj, proj, yf, gn_g.reshape(1, -1))


def _top2(logits):
    lane = lax.broadcasted_iota(jnp.int32, logits.shape, 1)
    v1 = jnp.max(logits, axis=-1, keepdims=True)
    i1 = jnp.min(jnp.where(logits == v1, lane, LANES), axis=-1, keepdims=True)
    rest = jnp.where(lane == i1, -jnp.inf, logits)
    v2 = jnp.max(rest, axis=-1, keepdims=True)
    i2 = jnp.min(jnp.where(rest == v2, lane, LANES), axis=-1, keepdims=True)
    e = jnp.exp(v2 - v1)
    w1 = 1.0 / (1.0 + e)
    w2 = e / (1.0 + e)
    idx = jnp.where(lane == 0, i1, jnp.where(lane == 1, i2, 0))
    w = jnp.where(lane == 0, w1, jnp.where(lane == 1, w2, 0.0))
    return idx, w


def _outproj_kernel(u_ref, w_ref, x_ref, gate_ref, g_ref, sh_ref, sc_ref, *rest, router):
    if router:
        wr_ref, br_ref, xo_ref, ho_ref, idx_ref, rw_ref = rest
    else:
        xo_ref, ho_ref = rest
    m = jnp.dot(u_ref[...], w_ref[...], preferred_element_type=F32)
    x1 = x_ref[...] + gate_ref[...] * m
    xo_ref[...] = x1
    h = _rms_mod(x1, g_ref[...], sh_ref[...], sc_ref[...])
    ho_ref[...] = h.astype(ho_ref.dtype)
    if router:
        logits = jnp.dot(h, wr_ref[...], precision=HIGHEST, preferred_element_type=F32) + br_ref[...]
        idx, w = _top2(logits)
        idx_ref[...] = idx
        rw_ref[...] = w


def _outproj(u, w_bf16, x, gate, g, sh, sc, seq, router=None, h_dtype=BF16):
    t, d = x.shape
    kdim = u.shape[1]
    tm = TOKEN_TILE
    per_b = seq // tm
    row = lambda width: pl.BlockSpec((tm, width), lambda i: (i, 0))
    vec = lambda: pl.BlockSpec((1, d), lambda i: (0, 0))
    bvec = lambda: pl.BlockSpec((None, 1, d), lambda i: (i // per_b, 0, 0))
    in_specs = [row(kdim), _resident((kdim, d)), row(d), bvec(), vec(), bvec(), bvec()]
    args = [u, w_bf16, x, gate, g.reshape(1, d), sh, sc]
    out_specs = [row(d), row(d)]
    out_shape = [jax.ShapeDtypeStruct((t, d), F32), jax.ShapeDtypeStruct((t, d), h_dtype)]
    if router is not None:
        wr, br = router
        in_specs += [_resident((d, LANES)), pl.BlockSpec((1, LANES), lambda i: (0, 0))]
        args += [wr, br]
        out_specs += [row(LANES), row(LANES)]
        out_shape += [jax.ShapeDtypeStruct((t, LANES), jnp.int32), jax.ShapeDtypeStruct((t, LANES), F32)]
    return pl.pallas_call(
        functools.partial(_outproj_kernel, router=router is not None),
        grid=(t // tm,),
        in_specs=in_specs, out_specs=out_specs, out_shape=out_shape,
        compiler_params=_params(),
        name="outproj_router" if router is not None else "outproj",
    )(*args)


def _ffn0_kernel(h_ref, wg_ref, wu_ref, wd_ref, x_ref, gate_ref, g_ref, sh_ref, sc_ref, xo_ref, ho_ref, *, tf):
    h = h_ref[...]
    acc = jnp.zeros(x_ref.shape, F32)
    for c in range(wg_ref.shape[1] // tf):
        cs = slice(c * tf, (c + 1) * tf)
        a = jnp.dot(h, wg_ref[:, cs], preferred_element_type=F32)
        b = jnp.dot(h, wu_ref[:, cs], preferred_element_type=F32)
        act = (_silu(a) * b).astype(BF16)
        acc = acc + jnp.dot(act, wd_ref[cs, :], preferred_element_type=F32)
    x2 = x_ref[...] + gate_ref[...] * acc
    xo_ref[...] = x2
    ho_ref[...] = _rms_mod(x2, g_ref[...], sh_ref[...], sc_ref[...]).astype(ho_ref.dtype)


def _ffn0(h, wg, wu, wd, x, gate, g, sh, sc, seq):
    t, d = x.shape
    f = wg.shape[1]
    tm = TOKEN_TILE
    per_b = seq // tm
    row = lambda: pl.BlockSpec((tm, d), lambda i: (i, 0))
    vec = lambda: pl.BlockSpec((1, d), lambda i: (0, 0))
    bvec = lambda: pl.BlockSpec((None, 1, d), lambda i: (i // per_b, 0, 0))
    return pl.pallas_call(
        functools.partial(_ffn0_kernel, tf=f // 2),
        grid=(t // tm,),
        in_specs=[row(), _resident((d, f)), _resident((d, f)), _resident((f, d)), row(), bvec(), vec(), bvec(), bvec()],
        out_specs=[row(), row()],
        out_shape=[jax.ShapeDtypeStruct((t, d), F32), jax.ShapeDtypeStruct((t, d), BF16)],
        compiler_params=_params(),
        name="l0_ffn",
    )(h, wg, wu, wd, x, gate, g.reshape(1, d), sh, sc)


def _qkv1_kernel(h_ref, w_ref, qg_ref, kg_ref, cr_ref, s1r_ref, s2r_ref, cc_ref, s1c_ref, s2c_ref, o_ref, *, tm):
    rows = tm // GRID_W
    y = jnp.dot(h_ref[...], w_ref[...], preferred_element_type=F32)
    cos =(cr_ref[...] + cc_ref[...][None]).reshape(tm, LANES)
    s1 = (s1r_ref[...] + s1c_ref[...][None]).reshape(tm, LANES)
    s2 = (s2r_ref[...] + s2c_ref[...][None]).reshape(tm, LANES)
    n_rot = ATT_HEADS + ATT_KV_HEADS
    for j in range(n_rot):
        ys = y[:, j * LANES:(j + 1) * LANES]
        gain = qg_ref[...] if j < ATT_HEADS else kg_ref[...]
        yn = ys * lax.rsqrt(jnp.mean(ys * ys, axis=-1, keepdims=True) + NORM_EPS) * gain
        o = yn * cos + pltpu.roll(yn, 96, 1) * s1 + pltpu.roll(yn, 32, 1) * s2
        if j < ATT_HEADS:
            o = o * (ATT_HD ** -0.5)
        o_ref[:, j * LANES:(j + 1) * LANES] = o.astype(BF16)
    o_ref[:, n_rot * LANES:] = y[:, n_rot * LANES:].astype(BF16)


def _att_rope_tables(seq):
    half = ATT_HD // 2
    nfreq = half // 2
    freqs = ROPE_THETA ** (-np.arange(0, nfreq, dtype=np.float64) * 2.0 / half)
    lane = np.arange(LANES)
    f_l = freqs[lane % nfreq]
    first = (lane % half) < nfreq
    is_row = lane < half
    n_img_rows = seq // GRID_W

    def tables(npos, mask):
        ang = np.arange(npos, dtype=np.float64)[:, None] * f_l[None, :]
        c = np.where(mask[None], np.cos(ang), 0.0)
        s1 = np.where((mask & first)[None], -np.sin(ang), 0.0)
        s2 = np.where((mask & ~first)[None], np.sin(ang), 0.0)
        return [a.astype(np.float32) for a in (c, s1, s2)]

    row_t = [jnp.asarray(a.reshape(n_img_rows, 1, LANES)) for a in tables(n_img_rows, is_row)]
    col_t = [jnp.asarray(a) for a in tables(GRID_W, ~is_row)]
    return row_t, col_t


def _qkv1(h, w_bf16, qg, kg, seq):
    t, d = h.shape
    n = w_bf16.shape[1]
    tm = TOKEN_TILE
    per_b = seq // tm
    rows = tm // GRID_W
    row_t, col_t = _att_rope_tables(seq)
    rowt = lambda: pl.BlockSpec((rows, 1, LANES), lambda i: (i % per_b, 0, 0))
    colt = lambda: pl.BlockSpec((GRID_W, LANES), lambda i: (0, 0))
    hvec = lambda: pl.BlockSpec((1, LANES), lambda i: (0, 0))
    return pl.pallas_call(
        functools.partial(_qkv1_kernel, tm=tm),
        grid=(t // tm,),
        in_specs=[pl.BlockSpec((tm, d), lambda i: (i, 0)), _resident((d, n)), hvec(), hvec(),
                  rowt(), rowt(), rowt(), colt(), colt(), colt()],
        out_specs=pl.BlockSpec((tm, n), lambda i: (i, 0)),
        out_shape=jax.ShapeDtypeStruct((t, n), BF16),
        compiler_params=_params(),
        name="l1_qkv",
    )(h, w_bf16, qg.reshape(1, LANES), kg.reshape(1, LANES), *row_t, *col_t)


def _flash_kernel(q_ref, k_ref, v_ref, o_ref, qs_ref, m_ref, l_ref, acc_ref, *, tq, tk):
    for g in range(ATT_GROUP):
        qs_ref[g * tq:(g + 1) * tq, :] = q_ref[:, g * LANES:(g + 1) * LANES]
    m_ref[...] = jnp.full(m_ref.shape, -jnp.inf, F32)
    l_ref[...] = jnp.zeros_like(l_ref)
    acc_ref[...] = jnp.zeros_like(acc_ref)

    def body(j, carry):
        ks = pl.ds(pl.multiple_of(j * tk, tk), tk)
        s = lax.dot_general(qs_ref[...], k_ref[ks, :], (((1,), (1,)), ((), ())), preferred_element_type=F32)
        m_old = m_ref[...]
        m_new = jnp.maximum(m_old, jnp.max(s, axis=-1, keepdims=True))
        p = jnp.exp(s - m_new)
        alpha = jnp.exp(m_old - m_new)
        l_ref[...] = alpha * l_ref[...] + jnp.sum(p, axis=-1, keepdims=True)
        acc_ref[...] = alpha * acc_ref[...] + jnp.dot(p.astype(BF16), v_ref[ks, :], preferred_element_type=F32)
        m_ref[...] = m_new
        return carry

    lax.fori_loop(0, k_ref.shape[0] // tk, body, 0)
    o = acc_ref[...] / l_ref[...]
    for g in range(ATT_GROUP):
        o_ref[:, g * LANES:(g + 1) * LANES] = o[g * tq:(g + 1) * tq, :].astype(o_ref.dtype)


def _attention(qkv, bsz, seq):
    t = qkv.shape[0]
    tq = ATT_Q_TILE
    tk = min(ATT_K_TILE, seq)
    nq = seq // tq
    gw = ATT_GROUP * ATT_HD
    m_rows = ATT_GROUP * tq
    return pl.pallas_call(
        functools.partial(_flash_kernel, tq=tq, tk=tk),
        grid=(bsz, ATT_KV_HEADS, nq),
        in_specs=[
            pl.BlockSpec((tq, gw), lambda b, kv, i: (b * nq + i, kv)),
            pl.BlockSpec((seq, ATT_HD), lambda b, kv, i: (b, ATT_HEADS + kv)),
            pl.BlockSpec((seq, ATT_HD), lambda b, kv, i: (b, ATT_HEADS + ATT_KV_HEADS + kv)),
        ],
        out_specs=pl.BlockSpec((tq, gw), lambda b, kv, i: (b * nq + i, kv)),
        out_shape=jax.ShapeDtypeStruct((t, ATT_HEADS * ATT_HD), BF16),
        scratch_shapes=[pltpu.VMEM((m_rows, ATT_HD), BF16), pltpu.VMEM((m_rows, 1), F32),
                        pltpu.VMEM((m_rows, 1), F32), pltpu.VMEM((m_rows, ATT_HD), F32)],
        compiler_params=_params(),
        name="l1_attention",
    )(qkv, qkv, qkv)


def _moe_ffn_kernel(te_ref, nu_ref, src_ref, h_hbm, wg_ref, wu_ref, wd_ref, y_ref, xg_ref, xb_ref, acc_ref, sem, *, tm):
    i = pl.program_id(0)
    j = pl.program_id(1)
    nj = pl.num_programs(1)
    used = i < nu_ref[0]

    @pl.when(jnp.logical_and(used, j == 0))
    def _():
        def issue(r, c):
            pltpu.make_async_copy(h_hbm.at[pl.ds(src_ref[0, r], 1), :], xg_ref.at[pl.ds(r, 1), :], sem).start()
            return c
        lax.fori_loop(0, tm, issue, 0)
        pltpu.make_async_copy(h_hbm.at[pl.ds(0, tm), :], xg_ref, sem).wait()
        xb_ref[...] = xg_ref[...].astype(BF16)
        acc_ref[...] = jnp.zeros_like(acc_ref)

    @pl.when(used)
    def _():
        xb = xb_ref[...]
        a = jnp.dot(xb, wg_ref[...], preferred_element_type=F32)
        b = jnp.dot(xb, wu_ref[...], preferred_element_type=F32)
        act = (_silu(a) * b).astype(BF16)
        acc_ref[...] += jnp.dot(act, wd_ref[...], preferred_element_type=F32)

    @pl.when(jnp.logical_and(used, j == nj - 1))
    def _():
        y_ref[...] = acc_ref[...].astype(y_ref.dtype)

    @pl.when(jnp.logical_and(jnp.logical_not(used), j == nj - 1))
    def _():
        y_ref[...] = jnp.zeros_like(y_ref)


def _moe_ffn(h, src, tile_expert, n_used, wg, wu, wd):
    t, d = h.shape
    n_tiles = src.shape[0]
    tm = MOE_TILE
    tf = MOE_FF_TILE
    f = wg.shape[2]
    nj = f // tf

    def jeff(i, j, nu):
        return jnp.where(i < nu[0], j, nj - 1)

    return pl.pallas_call(
        functools.partial(_moe_ffn_kernel, tm=tm),
        grid_spec=pltpu.PrefetchScalarGridSpec(
            num_scalar_prefetch=2, grid=(n_tiles, nj),
            in_specs=[
                pl.BlockSpec((None, 1, tm), lambda i, j, te, nu: (i, 0, 0), memory_space=pltpu.SMEM),
                pl.BlockSpec(memory_space=pl.ANY),
                pl.BlockSpec((None, d, tf), lambda i, j, te, nu: (te[i], 0, jeff(i, j, nu))),
                pl.BlockSpec((None, d, tf), lambda i, j, te, nu: (te[i], 0, jeff(i, j, nu))),
                pl.BlockSpec((None, tf, d), lambda i, j, te, nu: (te[i], jeff(i, j, nu), 0)),
            ],
            out_specs=pl.BlockSpec((tm, d), lambda i, j, te, nu: (i, 0)),
            scratch_shapes=[pltpu.VMEM((tm, d), F32), pltpu.VMEM((tm, d), BF16), pltpu.VMEM((tm, d), F32),
                            pltpu.SemaphoreType.DMA(())]),
        out_shape=jax.ShapeDtypeStruct((n_tiles * tm, d), F32),
        compiler_params=_params(),
        name="l1_moe_ffn",
    )(tile_expert, n_used, src, h, wg, wu, wd)


def _combine_kernel(dest_ref, y_hbm, x_ref, rw_ref, gate_ref, g_ref, sh_ref, sc_ref, o_ref, ya_ref, yb_ref, sem, *, tb):
    def issue(r, c):
        pltpu.make_async_copy(y_hbm.at[pl.ds(dest_ref[0, r], 1), :], ya_ref.at[pl.ds(r, 1), :], sem).start()
        pltpu.make_async_copy(y_hbm.at[pl.ds(dest_ref[1, r], 1), :], yb_ref.at[pl.ds(r, 1), :], sem).start()
        return c
    lax.fori_loop(0, tb, issue, 0)
    pltpu.make_async_copy(y_hbm.at[pl.ds(0, tb), :], ya_ref, sem).wait()
    pltpu.make_async_copy(y_hbm.at[pl.ds(0, tb), :], yb_ref, sem).wait()
    rw = rw_ref[...]
    f = rw[:, 0:1] * ya_ref[...] + rw[:, 1:2] * yb_ref[...]
    x = x_ref[...] + gate_ref[...] * f
    o_ref[...] = _rms_mod(x, g_ref[...], sh_ref[...], sc_ref[...])


def _combine(dest, ys, x, rw, gate, g, sh, sc, seq):
    t, d = x.shape
    tb = COMBINE_TILE
    per_b = seq // tb
    vec = lambda: pl.BlockSpec((1, d), lambda i: (0, 0))
    bvec = lambda: pl.BlockSpec((None, 1, d), lambda i: (i // per_b, 0, 0))
    return pl.pallas_call(
        functools.partial(_combine_kernel, tb=tb),
        grid=(t // tb,),
        in_specs=[
            pl.BlockSpec((None, 2, tb), lambda i: (i, 0, 0), memory_space=pltpu.SMEM),
            pl.BlockSpec(memory_space=pl.ANY),
            pl.BlockSpec((tb, d), lambda i: (i, 0)),
            pl.BlockSpec((tb, LANES), lambda i: (i, 0)),
            bvec(), vec(), bvec(), bvec(),
        ],
        out_specs=pl.BlockSpec((tb, d), lambda i: (i, 0)),
        out_shape=jax.ShapeDtypeStruct((t, d), F32),
        scratch_shapes=[pltpu.VMEM((tb, d), F32), pltpu.VMEM((tb, d), F32), pltpu.SemaphoreType.DMA(())],
        compiler_params=_params(),
        name="l1_moe_combine",
    )(dest, ys, x, rw, gate, g.reshape(1, d), sh, sc)


def _moe_plan(idx, t):
    tm = MOE_TILE
    n_tiles = (2 * t) // tm + N_EXPERTS
    e_flat = idx.reshape(-1)
    onehot = (e_flat[:, None] == jnp.arange(N_EXPERTS, dtype=jnp.int32)[None, :]).astype(jnp.int32)
    incl = jnp.cumsum(onehot, axis=0)
    counts = incl[-1]
    pos = jnp.sum((incl - onehot) * onehot, axis=1)
    tiles_per_e = (counts + tm - 1) // tm
    tile_end = jnp.cumsum(tiles_per_e)
    row_start = (tile_end - tiles_per_e) * tm
    dest = row_start[e_flat] + pos
    n_used = tile_end[-1:]
    tile_ids = jnp.arange(n_tiles, dtype=jnp.int32)
    tile_expert = jnp.minimum(jnp.sum((tile_ids[:, None] >= tile_end[None, :]).astype(jnp.int32), axis=1),
                              N_EXPERTS - 1)
    last_e = jnp.max(jnp.where(counts > 0, jnp.arange(N_EXPERTS), 0))
    tile_expert = jnp.where(tile_ids < n_used[0], tile_expert, last_e).astype(jnp.int32)
    src = jnp.zeros((n_tiles * tm,), jnp.int32).at[dest].set(jnp.arange(2 * t, dtype=jnp.int32) // 2)
    tb = COMBINE_TILE
    dest_t = dest.reshape(t // tb, tb, 2).transpose(0, 2, 1)
    return src.reshape(n_tiles, 1, tm), dest_t, tile_expert, n_used.astype(jnp.int32)


def kernel(x, c, l0_ada_w, l0_ada_b, l0_norm1_g, l0_norm2_g, l0_ret_w_in, l0_ret_decay_logit, l0_ret_gn_g, l0_ret_w_out, l0_ffn_w_gate, l0_ffn_w_up, l0_ffn_w_down, l1_ada_w, l1_ada_b, l1_norm1_g, l1_norm2_g, l1_attn_w_qkv, l1_attn_q_norm_g, l1_attn_k_norm_g, l1_attn_w_out, l1_moe_w_router, l1_moe_b_router, l1_moe_w_gate, l1_moe_w_up, l1_moe_w_down, final_ada_w, final_ada_b, final_norm_g):
    bsz, seq, d = x.shape
    t = bsz * seq
    xf = x.reshape(t, d)

    c8 = jnp.zeros((8, d), F32).at[:bsz].set(c)

    def mods(w, b, n):
        m = _adaln(c8, w, b)[:bsz]
        return [m[:, i * d:(i + 1) * d].reshape(bsz, 1, d) for i in range(n)]

    sh1, sc1, g1, sh2, sc2, g2 = mods(l0_ada_w, l0_ada_b, 6)
    sh3, sc3, g3, sh4, sc4, g4 = mods(l1_ada_w, l1_ada_b, 6)
    fsh, fsc = mods(final_ada_w, final_ada_b, 2)

    bf = lambda w: w.astype(BF16)

    proj = _inproj0(xf, l0_norm1_g, sh1, sc1, bf(l0_ret_w_in), seq)
    log_gamma = jax.nn.log_sigmoid(l0_ret_decay_logit.astype(F32))
    u = _retention(proj, log_gamma, l0_ret_gn_g, bsz, seq)
    x1, h2 = _outproj(u, bf(l0_ret_w_out), xf, g1, l0_norm2_g, sh2, sc2, seq)
    x2, h3 = _ffn0(h2, bf(l0_ffn_w_gate), bf(l0_ffn_w_up), bf(l0_ffn_w_down), x1, g2, l1_norm1_g, sh3, sc3, seq)

    qkv = _qkv1(h3, bf(l1_attn_w_qkv), l1_attn_q_norm_g, l1_attn_k_norm_g, seq)
    o = _attention(qkv, bsz, seq)
    wr = jnp.zeros((d, LANES), F32).at[:, :N_EXPERTS].set(l1_moe_w_router)
    br = jnp.full((1, LANES), -jnp.inf, F32).at[0, :N_EXPERTS].set(l1_moe_b_router.astype(F32))
    x3, h4, idx, rw = _outproj(o, bf(l1_attn_w_out), x2, g3, l1_norm2_g, sh4, sc4, seq,
                               router=(wr, br), h_dtype=F32)
    src, dest, tile_expert, n_used = _moe_plan(idx[:, :2], t)
    ys = _moe_ffn(h4, src, tile_expert, n_used, bf(l1_moe_w_gate), bf(l1_moe_w_up), bf(l1_moe_w_down))
    out = _combine(dest, ys, x3, rw, g4, final_norm_g, fsh, fsc, seq)
    return out.reshape(bsz, seq, d)
```

```python
import functools

import numpy as np
import jax
import jax.numpy as jnp
from jax import lax
from jax.experimental import pallas as pl
from jax.experimental.pallas import tpu as pltpu

GRID_W = 64
NORM_EPS = 1e-6
ROPE_THETA = 10000.0
RET_HEADS = 4
RET_DK = 256
RET_DV = 512
RET_CHUNK = 128
ATT_HEADS = 8
ATT_KV_HEADS = 2
ATT_HD = 128
ATT_GROUP = ATT_HEADS // ATT_KV_HEADS
N_EXPERTS = 8
LANES = 128
LOG2_E = 1.4426950408889634

F32 = jnp.float32
BF16 = jnp.bfloat16
HIGHEST = lax.Precision.HIGHEST

TOKEN_TILE = 512
RET_BLOCK = 512
ATT_Q_TILE = 256
ATT_K_TILE = 512
MOE_TILE = 512
MOE_FF_TILE = 512
COMBINE_TILE = 256
VMEM_LIMIT = 56 * 1024 * 1024


def _params(**kw):
    return pltpu.CompilerParams(vmem_limit_bytes=VMEM_LIMIT, **kw)


def _resident(shape):
    return pl.BlockSpec(shape, lambda *_: (0,) * len(shape), pipeline_mode=pl.Buffered(1))


def _silu(a):
    return a * jax.nn.sigmoid(a)


def _rms_mod(x, g, shift, scale):
    y = x * lax.rsqrt(jnp.mean(x * x, axis=-1, keepdims=True) + NORM_EPS)
    return (y * g) * (1.0 + scale) + shift


def _adaln_kernel(c_ref, w_ref, b_ref, o_ref):
    a = _silu(c_ref[...])
    o_ref[...] = jnp.dot(a, w_ref[...], precision=HIGHEST, preferred_element_type=F32) + b_ref[...]


def _adaln(c8, w, b):
    d, n = w.shape
    tn = 1024
    return pl.pallas_call(
        _adaln_kernel,
        grid=(n // tn,),
        in_specs=[
            pl.BlockSpec((8, d), lambda j: (0, 0)),
            pl.BlockSpec((d, tn), lambda j: (0, j)),
            pl.BlockSpec((1, tn), lambda j: (0, j)),
        ],
        out_specs=pl.BlockSpec((8, tn), lambda j: (0, j)),
        out_shape=jax.ShapeDtypeStruct((8, n), F32),
        compiler_params=_params(),
        name="adaln",
    )(c8, w, b.reshape(1, n))


def _rope_tables(seq, half):
    nfreq = half // 2
    freqs = ROPE_THETA ** (-np.arange(0, nfreq, dtype=np.float64) * 2.0 / half)
    npos = max(seq // GRID_W, GRID_W)
    ang = np.arange(npos, dtype=np.float64)[:, None] * freqs[None, :]
    cos = np.concatenate([np.cos(ang), np.cos(ang)], axis=1)
    sin = np.concatenate([-np.sin(ang), np.sin(ang)], axis=1)
    return cos.astype(np.float32), sin.astype(np.float32)


def _inproj0_kernel(x_ref, g_ref, sh_ref, sc_ref, w_ref, cr_ref, sr_ref, cc_ref, scol_ref, o_ref, *, tm):
    h = _rms_mod(x_ref[...], g_ref[...], sh_ref[...], sc_ref[...]).astype(BF16)
    rows = tm // GRID_W
    cr = cr_ref[...]
    sr = sr_ref[...]
    cc = cc_ref[...][None]
    scol = scol_ref[...][None]
    nw = 512
    n_qk = 2 * RET_HEADS * RET_DK
    for c in range(o_ref.shape[1] // nw):
        y = jnp.dot(h, w_ref[:, c * nw:(c + 1) * nw], preferred_element_type=F32)
        if c * nw < n_qk:
            is_k = c * nw >= RET_HEADS * RET_DK
            for s in range(nw // LANES):
                ys = y[:, s * LANES:(s + 1) * LANES]
                y3 = ys.reshape(rows, GRID_W, LANES)
                r3 = pltpu.roll(ys, LANES // 2, 1).reshape(rows, GRID_W, LANES)
                if s % 2 == 0:
                    o3 = y3 * cr + r3 * sr
                else:
                    o3 = y3 * cc + r3 * scol
                if is_k:
                    o3 = o3 * (RET_DK ** -0.5)
                o_ref[:, c * nw + s * LANES:c * nw + (s + 1) * LANES] = o3.reshape(tm, LANES).astype(BF16)
        else:
            o_ref[:, c * nw:(c + 1) * nw] = y.astype(BF16)


def _inproj0(x, g, sh, sc, w_bf16, seq):
    t, d = x.shape
    n = w_bf16.shape[1]
    tm = TOKEN_TILE
    per_b = seq // tm
    rows = tm // GRID_W
    cos, sin = _rope_tables(seq, LANES)
    n_img_rows = seq // GRID_W
    cr = jnp.asarray(cos[:n_img_rows].reshape(n_img_rows, 1, LANES))
    sr = jnp.asarray(sin[:n_img_rows].reshape(n_img_rows, 1, LANES))
    cc = jnp.asarray(cos[:GRID_W])
    scol = jnp.asarray(sin[:GRID_W])
    vec = lambda: pl.BlockSpec((1, d), lambda i: (0, 0))
    bvec = lambda: pl.BlockSpec((None, 1, d), lambda i: (i // per_b, 0, 0))
    rowt = lambda: pl.BlockSpec((rows, 1, LANES), lambda i: (i % per_b, 0, 0))
    colt = lambda: pl.BlockSpec((GRID_W, LANES), lambda i: (0, 0))
    return pl.pallas_call(
        functools.partial(_inproj0_kernel, tm=tm),
        grid=(t // tm,),
        in_specs=[pl.BlockSpec((tm, d), lambda i: (i, 0)), vec(), bvec(), bvec(),
                  _resident((d, n)), rowt(), rowt(), colt(), colt()],
        out_specs=pl.BlockSpec((tm, n), lambda i: (i, 0)),
        out_shape=jax.ShapeDtypeStruct((t, n), BF16),
        compiler_params=_params(),
        name="l0_inproj",
    )(x, g.reshape(1, d), sh, sc, w_bf16, cr, sr, cc, scol)


def _col(n):
    return lax.broadcasted_iota(jnp.int32, (n, 1), 0).astype(F32)


def _ret_fwd_kernel(lg_ref, q_ref, k_ref, v_ref, y_ref, state_ref, *, nchunks):
    hd = pl.program_id(1)

    @pl.when(pl.program_id(2) == 0)
    def _():
        state_ref[...] = jnp.zeros_like(state_ref)

    c_len = RET_CHUNK
    lgf = lg_ref[0, hd]
    lgb = lg_ref[1, hd]
    ii = lax.broadcasted_iota(jnp.int32, (c_len, c_len), 0)
    jj = lax.broadcasted_iota(jnp.int32, (c_len, c_len), 1)
    diff = (ii - jj).astype(F32)
    dmat = jnp.where(diff >= 0, jnp.exp(diff * lgf), jnp.exp(-diff * lgb))
    idx = _col(c_len)
    xi = jnp.exp((idx + 1.0) * lgf)
    zeta = jnp.exp((c_len - 1.0 - idx) * lgf)
    decay = jnp.exp(jnp.full((1, 1), c_len, F32) * lgf)
    for c in range(nchunks):
        sl = slice(c * c_len, (c + 1) * c_len)
        q = q_ref[sl, :]
        k = k_ref[sl, :]
        v = v_ref[sl, :]
        s = lax.dot_general(q, k, (((1,), (1,)), ((), ())), preferred_element_type=F32) * dmat
        y = jnp.dot(s.astype(BF16), v, preferred_element_type=F32)
        st = state_ref[...]
        y = y + jnp.dot(q, st.astype(BF16), preferred_element_type=F32) * xi
        kz = (k.astype(F32) * zeta).astype(BF16)
        upd = lax.dot_general(kz, v, (((0,), (0,)), ((), ())), preferred_element_type=F32)
        state_ref[...] = st * decay + upd
        y_ref[sl, :] = y.astype(y_ref.dtype)


def _ret_bwd_kernel(lg_ref, q_ref, k_ref, v_ref, g_ref, yf_ref, gn_ref, u_ref, state_ref, *, nchunks):
    hd = pl.program_id(1)

    @pl.when(pl.program_id(2) == 0)
    def _():
        state_ref[...] = jnp.zeros_like(state_ref)

    c_len = RET_CHUNK
    lgb = lg_ref[1, hd]
    idx = _col(c_len)
    xi = jnp.exp((c_len - idx) * lgb)
    zeta = jnp.exp(idx * lgb)
    decay = jnp.exp(jnp.full((1, 1), c_len, F32) * lgb)
    gn = gn_ref[...]
    for c in reversed(range(nchunks)):
        sl = slice(c * c_len, (c + 1) * c_len)
        q = q_ref[sl, :]
        k = k_ref[sl, :]
        v = v_ref[sl, :]
        st = state_ref[...]
        y = yf_ref[sl, :].astype(F32) + jnp.dot(q, st.astype(BF16), preferred_element_type=F32) * xi
        kz = (k.astype(F32) * zeta).astype(BF16)
        upd = lax.dot_general(kz, v, (((0,), (0,)), ((), ())), preferred_element_type=F32)
        state_ref[...] = st * decay + upd
        mu = jnp.mean(y, axis=-1, keepdims=True)
        yc = y - mu
        var = jnp.mean(yc * yc, axis=-1, keepdims=True)
        yn = yc * lax.rsqrt(var + NORM_EPS) * gn
        u_ref[sl, :] = (_silu(g_ref[sl, :].astype(F32)) * yn).astype(u_ref.dtype)


def _retention(proj, log_gamma, gn_g, bsz, seq):
    t = proj.shape[0]
    tb = RET_BLOCK
    nb = seq // tb
    nchunks = tb // RET_CHUNK
    hk = RET_HEADS
    grid = (bsz, RET_HEADS, nb)
    qspec = lambda f: pl.BlockSpec((tb, RET_DK), lambda b, h, n, lg: (b * nb + f(n), h))
    kspec = lambda f: pl.BlockSpec((tb, RET_DK), lambda b, h, n, lg: (b * nb + f(n), hk + h))
    vspec = lambda f: pl.BlockSpec((tb, RET_DV), lambda b, h, n, lg: (b * nb + f(n), hk + h))
    gspec = lambda f: pl.BlockSpec((tb, RET_DV), lambda b, h, n, lg: (b * nb + f(n), 2 * hk + h))
    yspec = lambda f: pl.BlockSpec((tb, RET_DV), lambda b, h, n, lg: (b * nb + f(n), h))
    fwd = lambda n: n
    rev = lambda n: nb - 1 - n
    state = pltpu.VMEM((RET_DK, RET_DV), F32)
    yf = pl.pallas_call(
        functools.partial(_ret_fwd_kernel, nchunks=nchunks),
        grid_spec=pltpu.PrefetchScalarGridSpec(
            num_scalar_prefetch=1, grid=grid,
            in_specs=[qspec(fwd), kspec(fwd), vspec(fwd)],
            out_specs=yspec(fwd), scratch_shapes=[state]),
        out_shape=jax.ShapeDtypeStruct((t, RET_HEADS * RET_DV), BF16),
        compiler_params=_params(),
        name="l0_ret_fwd",
    )(log_gamma, proj, proj, proj)
    return pl.pallas_call(
        functools.partial(_ret_bwd_kernel, nchunks=nchunks),
        grid_spec=pltpu.PrefetchScalarGridSpec(
            num_scalar_prefetch=1, grid=grid,
            in_specs=[qspec(rev), kspec(rev), vspec(rev), gspec(rev), yspec(rev),
                      pl.BlockSpec((1, RET_DV), lambda b, h, n, lg: (0, h))],
            out_specs=yspec(rev), scratch_shapes=[state]),
        out_shape=jax.ShapeDtypeStruct((t, RET_HEADS * RET_DV), BF16),
        compiler_params=_params(),
        name="l0_ret_bwd",
    )(log_gamma, proj, proj, proj, proj, yf, gn_g.reshape(1, -1))


def _top2(logits):
    lane = lax.broadcasted_iota(jnp.int32, logits.shape, 1)
    v1 = jnp.max(logits, axis=-1, keepdims=True)
    i1 = jnp.min(jnp.where(logits == v1, lane, LANES), axis=-1, keepdims=True)
    rest = jnp.where(lane == i1, -jnp.inf, logits)
    v2 = jnp.max(rest, axis=-1, keepdims=True)
    i2 = jnp.min(jnp.where(rest == v2, lane, LANES), axis=-1, keepdims=True)
    e = jnp.exp(v2 - v1)
    w1 = 1.0 / (1.0 + e)
    w2 = e / (1.0 + e)
    idx = jnp.where(lane == 0, i1, jnp.where(lane == 1, i2, 0))
    w = jnp.where(lane == 0, w1, jnp.where(lane == 1, w2, 0.0))
    return idx, w


def _outproj_kernel(u_ref, w_ref, x_ref, gate_ref, g_ref, sh_ref, sc_ref, *rest, router):
    if router:
        wr_ref, br_ref, xo_ref, ho_ref, idx_ref, rw_ref = rest
    else:
        xo_ref, ho_ref = rest
    m = jnp.dot(u_ref[...], w_ref[...], preferred_element_type=F32)
    x1 = x_ref[...] + gate_ref[...] * m
    xo_ref[...] = x1
    h = _rms_mod(x1, g_ref[...], sh_ref[...], sc_ref[...])
    ho_ref[...] = h.astype(ho_ref.dtype)
    if router:
        logits = jnp.dot(h, wr_ref[...], precision=HIGHEST, preferred_element_type=F32) + br_ref[...]
        idx, w = _top2(logits)
        idx_ref[...] = idx
        rw_ref[...] = w


def _outproj(u, w_bf16, x, gate, g, sh, sc, seq, router=None, h_dtype=BF16):
    t, d = x.shape
    kdim = u.shape[1]
    tm = TOKEN_TILE
    per_b = seq // tm
    row = lambda width: pl.BlockSpec((tm, width), lambda i: (i, 0))
    vec = lambda: pl.BlockSpec((1, d), lambda i: (0, 0))
    bvec = lambda: pl.BlockSpec((None, 1, d), lambda i: (i // per_b, 0, 0))
    in_specs = [row(kdim), _resident((kdim, d)), row(d), bvec(), vec(), bvec(), bvec()]
    args = [u, w_bf16, x, gate, g.reshape(1, d), sh, sc]
    out_specs = [row(d), row(d)]
    out_shape = [jax.ShapeDtypeStruct((t, d), F32), jax.ShapeDtypeStruct((t, d), h_dtype)]
    if router is not None:
        wr, br = router
        in_specs += [_resident((d, LANES)), pl.BlockSpec((1, LANES), lambda i: (0, 0))]
        args += [wr, br]
        out_specs += [row(LANES), row(LANES)]
        out_shape += [jax.ShapeDtypeStruct((t, LANES), jnp.int32), jax.ShapeDtypeStruct((t, LANES), F32)]
    return pl.pallas_call(
        functools.partial(_outproj_kernel, router=router is not None),
        grid=(t // tm,),
        in_specs=in_specs, out_specs=out_specs, out_shape=out_shape,
        compiler_params=_params(),
        name="outproj_router" if router is not None else "outproj",
    )(*args)


def _ffn0_kernel(h_ref, wg_ref, wu_ref, wd_ref, x_ref, gate_ref, g_ref, sh_ref, sc_ref, xo_ref, ho_ref, *, tf):
    h = h_ref[...]
    acc = jnp.zeros(x_ref.shape, F32)
    for c in range(wg_ref.shape[1] // tf):
        cs = slice(c * tf, (c + 1) * tf)
        a = jnp.dot(h, wg_ref[:, cs], preferred_element_type=F32)
        b = jnp.dot(h, wu_ref[:, cs], preferred_element_type=F32)
        act = (_silu(a) * b).astype(BF16)
        acc = acc + jnp.dot(act, wd_ref[cs, :], preferred_element_type=F32)
    x2 = x_ref[...] + gate_ref[...] * acc
    xo_ref[...] = x2
    ho_ref[...] = _rms_mod(x2, g_ref[...], sh_ref[...], sc_ref[...]).astype(ho_ref.dtype)


def _ffn0(h, wg, wu, wd, x, gate, g, sh, sc, seq):
    t, d = x.shape
    f = wg.shape[1]
    tm = TOKEN_TILE
    per_b = seq // tm
    row = lambda: pl.BlockSpec((tm, d), lambda i: (i, 0))
    vec = lambda: pl.BlockSpec((1, d), lambda i: (0, 0))
    bvec = lambda: pl.BlockSpec((None, 1, d), lambda i: (i // per_b, 0, 0))
    return pl.pallas_call(
        functools.partial(_ffn0_kernel, tf=f // 2),
        grid=(t // tm,),
        in_specs=[row(), _resident((d, f)), _resident((d, f)), _resident((f, d)), row(), bvec(), vec(), bvec(), bvec()],
        out_specs=[row(), row()],
        out_shape=[jax.ShapeDtypeStruct((t, d), F32), jax.ShapeDtypeStruct((t, d), BF16)],
        compiler_params=_params(),
        name="l0_ffn",
    )(h, wg, wu, wd, x, gate, g.reshape(1, d), sh, sc)


def _qkv1_kernel(h_ref, w_ref, qg_ref, kg_ref, cr_ref, s1r_ref, s2r_ref, cc_ref, s1c_ref, s2c_ref, o_ref, *, tm):
    rows = tm // GRID_W
    y = jnp.dot(h_ref[...], w_ref[...], preferred_element_type=F32)
    cos =(cr_ref[...] + cc_ref[...][None]).reshape(tm, LANES)
    s1 = (s1r_ref[...] + s1c_ref[...][None]).reshape(tm, LANES)
    s2 = (s2r_ref[...] + s2c_ref[...][None]).reshape(tm, LANES)
    n_rot = ATT_HEADS + ATT_KV_HEADS
    for j in range(n_rot):
        ys = y[:, j * LANES:(j + 1) * LANES]
        gain = qg_ref[...] if j < ATT_HEADS else kg_ref[...]
        yn = ys * lax.rsqrt(jnp.mean(ys * ys, axis=-1, keepdims=True) + NORM_EPS) * gain
        o = yn * cos + pltpu.roll(yn, 96, 1) * s1 + pltpu.roll(yn, 32, 1) * s2
        if j < ATT_HEADS:
            o = o * (ATT_HD ** -0.5 * LOG2_E)
        o_ref[:, j * LANES:(j + 1) * LANES] = o.astype(BF16)
    o_ref[:, n_rot * LANES:] = y[:, n_rot * LANES:].astype(BF16)


def _att_rope_tables(seq):
    half = ATT_HD // 2
    nfreq = half // 2
    freqs = ROPE_THETA ** (-np.arange(0, nfreq, dtype=np.float64) * 2.0 / half)
    lane = np.arange(LANES)
    f_l = freqs[lane % nfreq]
    first = (lane % half) < nfreq
    is_row = lane < half
    n_img_rows = seq // GRID_W

    def tables(npos, mask):
        ang = np.arange(npos, dtype=np.float64)[:, None] * f_l[None, :]
        c = np.where(mask[None], np.cos(ang), 0.0)
        s1 = np.where((mask & first)[None], -np.sin(ang), 0.0)
        s2 = np.where((mask & ~first)[None], np.sin(ang), 0.0)
        return [a.astype(np.float32) for a in (c, s1, s2)]

    row_t = [jnp.asarray(a.reshape(n_img_rows, 1, LANES)) for a in tables(n_img_rows, is_row)]
    col_t = [jnp.asarray(a) for a in tables(GRID_W, ~is_row)]
    return row_t, col_t


def _qkv1(h, w_bf16, qg, kg, seq):
    t, d = h.shape
    n = w_bf16.shape[1]
    tm = TOKEN_TILE
    per_b = seq // tm
    rows = tm // GRID_W
    row_t, col_t = _att_rope_tables(seq)
    rowt = lambda: pl.BlockSpec((rows, 1, LANES), lambda i: (i % per_b, 0, 0))
    colt = lambda: pl.BlockSpec((GRID_W, LANES), lambda i: (0, 0))
    hvec = lambda: pl.BlockSpec((1, LANES), lambda i: (0, 0))
    return pl.pallas_call(
        functools.partial(_qkv1_kernel, tm=tm),
        grid=(t // tm,),
        in_specs=[pl.BlockSpec((tm, d), lambda i: (i, 0)), _resident((d, n)), hvec(), hvec(),
                  rowt(), rowt(), rowt(), colt(), colt(), colt()],
        out_specs=pl.BlockSpec((tm, n), lambda i: (i, 0)),
        out_shape=jax.ShapeDtypeStruct((t, n), BF16),
        compiler_params=_params(),
        name="l1_qkv",
    )(h, w_bf16, qg.reshape(1, LANES), kg.reshape(1, LANES), *row_t, *col_t)


def _flash_kernel(q_ref, k_ref, vt_ref, o_ref, qs_ref, m_ref, l_ref, acc_ref, s_ref, *, tq, tk):
    for g in range(ATT_GROUP):
        qs_ref[g * tq:(g + 1) * tq, :] = q_ref[:, g * LANES:(g + 1) * LANES]
    m_ref[...] = jnp.full(m_ref.shape, -jnp.inf, F32)
    l_ref[...] = jnp.zeros_like(l_ref)
    acc_ref[...] = jnp.zeros_like(acc_ref)
    nk = vt_ref.shape[0]

    def scores(j, slot):
        k_c = k_ref[pl.ds(pl.multiple_of(j * tk, tk), tk), :]
        s_ref[slot] = lax.dot_general(k_c, qs_ref[...], (((1,), (1,)), ((), ())), preferred_element_type=F32)

    def update(j, slot):
        s = s_ref[slot]
        m_old = m_ref[...]
        m_new = jnp.maximum(m_old, jnp.max(s, axis=0, keepdims=True))
        p = jnp.exp2(s - m_new)
        alpha = jnp.exp2(m_old - m_new)
        l_ref[...] = alpha * l_ref[...] + jnp.sum(p, axis=0, keepdims=True)
        acc_ref[...] = alpha * acc_ref[...] + jnp.dot(vt_ref[j], p.astype(BF16), preferred_element_type=F32)
        m_ref[...] = m_new

    scores(0, 0)

    def body(jj, carry):
        j = 2 * jj
        scores(j + 1, 1)
        update(j, 0)
        scores(jnp.minimum(j + 2, nk - 1), 0)
        update(j + 1, 1)
        return carry

    lax.fori_loop(0, nk // 2, body, 0)
    o = acc_ref[...] / l_ref[...]
    for g in range(ATT_GROUP):
        o_ref[:, g * LANES:(g + 1) * LANES] = o[:, g * tq:(g + 1) * tq].T.astype(o_ref.dtype)


def _attention(qkv, bsz, seq):
    t = qkv.shape[0]
    tq = ATT_Q_TILE
    tk = min(ATT_K_TILE, seq)
    nq = seq // tq
    nk = seq // tk
    gw = ATT_GROUP * ATT_HD
    nqs = ATT_GROUP * tq
    v = qkv[:, (ATT_HEADS + ATT_KV_HEADS) * ATT_HD:]
    vt = v.reshape(bsz, nk, tk, ATT_KV_HEADS, ATT_HD).transpose(0, 3, 1, 4, 2)
    return pl.pallas_call(
        functools.partial(_flash_kernel, tq=tq, tk=tk),
        grid=(bsz, ATT_KV_HEADS, nq),
        in_specs=[
            pl.BlockSpec((tq, gw), lambda b, kv, i: (b * nq + i, kv)),
            pl.BlockSpec((seq, ATT_HD), lambda b, kv, i: (b, ATT_HEADS + kv)),
            pl.BlockSpec((None, None, nk, ATT_HD, tk), lambda b, kv, i: (b, kv, 0, 0, 0)),
        ],
        out_specs=pl.BlockSpec((tq, gw), lambda b, kv, i: (b * nq + i, kv)),
        out_shape=jax.ShapeDtypeStruct((t, ATT_HEADS * ATT_HD), BF16),
        scratch_shapes=[pltpu.VMEM((nqs, ATT_HD), BF16), pltpu.VMEM((1, nqs), F32), pltpu.VMEM((1, nqs), F32),
                        pltpu.VMEM((ATT_HD, nqs), F32), pltpu.VMEM((2, tk, nqs), F32)],
        compiler_params=_params(),
        name="l1_attention",
    )(qkv, qkv, vt)


def _moe_ffn_kernel(te_ref, nu_ref, src_ref, h_hbm, wg_ref, wu_ref, wd_ref, y_ref, xg_ref, xb_ref, acc_ref, sem, *, tm):
    i = pl.program_id(0)
    j = pl.program_id(1)
    nj = pl.num_programs(1)
    used = i < nu_ref[0]

    @pl.when(jnp.logical_and(used, j == 0))
    def _():
        def issue(r, c):
            pltpu.make_async_copy(h_hbm.at[pl.ds(src_ref[0, r], 1), :], xg_ref.at[pl.ds(r, 1), :], sem).start()
            return c
        lax.fori_loop(0, tm, issue, 0)
        pltpu.make_async_copy(h_hbm.at[pl.ds(0, tm), :], xg_ref, sem).wait()
        xb_ref[...] = xg_ref[...].astype(BF16)
        acc_ref[...] = jnp.zeros_like(acc_ref)

    @pl.when(used)
    def _():
        xb = xb_ref[...]
        a = jnp.dot(xb, wg_ref[...], preferred_element_type=F32)
        b = jnp.dot(xb, wu_ref[...], preferred_element_type=F32)
        act = (_silu(a) * b).astype(BF16)
        acc_ref[...] += jnp.dot(act, wd_ref[...], preferred_element_type=F32)

    @pl.when(jnp.logical_and(used, j == nj - 1))
    def _():
        y_ref[...] = acc_ref[...].astype(y_ref.dtype)

    @pl.when(jnp.logical_and(jnp.logical_not(used), j == nj - 1))
    def _():
        y_ref[...] = jnp.zeros_like(y_ref)


def _moe_ffn(h, src, tile_expert, n_used, wg, wu, wd):
    t, d = h.shape
    n_tiles = src.shape[0]
    tm = MOE_TILE
    tf = MOE_FF_TILE
    f = wg.shape[2]
    nj = f // tf

    def jeff(i, j, nu):
        return jnp.where(i < nu[0], j, nj - 1)

    return pl.pallas_call(
        functools.partial(_moe_ffn_kernel, tm=tm),
        grid_spec=pltpu.PrefetchScalarGridSpec(
            num_scalar_prefetch=2, grid=(n_tiles, nj),
            in_specs=[
                pl.BlockSpec((None, 1, tm), lambda i, j, te, nu: (i, 0, 0), memory_space=pltpu.SMEM),
                pl.BlockSpec(memory_space=pl.ANY),
                pl.BlockSpec((None, d, tf), lambda i, j, te, nu: (te[i], 0, jeff(i, j, nu))),
                pl.BlockSpec((None, d, tf), lambda i, j, te, nu: (te[i], 0, jeff(i, j, nu))),
                pl.BlockSpec((None, tf, d), lambda i, j, te, nu: (te[i], jeff(i, j, nu), 0)),
            ],
            out_specs=pl.BlockSpec((tm, d), lambda i, j, te, nu: (i, 0)),
            scratch_shapes=[pltpu.VMEM((tm, d), F32), pltpu.VMEM((tm, d), BF16), pltpu.VMEM((tm, d), F32),
                            pltpu.SemaphoreType.DMA(())]),
        out_shape=jax.ShapeDtypeStruct((n_tiles * tm, d), F32),
        compiler_params=_params(),
        name="l1_moe_ffn",
    )(tile_expert, n_used, src, h, wg, wu, wd)


def _combine_kernel(dest_ref, y_hbm, x_ref, rw_ref, gate_ref, g_ref, sh_ref, sc_ref, o_ref, ya_ref, yb_ref, sem, *, tb):
    def issue(r, c):
        pltpu.make_async_copy(y_hbm.at[pl.ds(dest_ref[0, r], 1), :], ya_ref.at[pl.ds(r, 1), :], sem).start()
        pltpu.make_async_copy(y_hbm.at[pl.ds(dest_ref[1, r], 1), :], yb_ref.at[pl.ds(r, 1), :], sem).start()
        return c
    lax.fori_loop(0, tb, issue, 0)
    pltpu.make_async_copy(y_hbm.at[pl.ds(0, tb), :], ya_ref, sem).wait()
    pltpu.make_async_copy(y_hbm.at[pl.ds(0, tb), :], yb_ref, sem).wait()
    rw = rw_ref[...]
    f = rw[:, 0:1] * ya_ref[...] + rw[:, 1:2] * yb_ref[...]
    x = x_ref[...] + gate_ref[...] * f
    o_ref[...] = _rms_mod(x, g_ref[...], sh_ref[...], sc_ref[...])


def _combine(dest, ys, x, rw, gate, g, sh, sc, seq):
    t, d = x.shape
    tb = COMBINE_TILE
    per_b = seq // tb
    vec = lambda: pl.BlockSpec((1, d), lambda i: (0, 0))
    bvec = lambda: pl.BlockSpec((None, 1, d), lambda i: (i // per_b, 0, 0))
    return pl.pallas_call(
        functools.partial(_combine_kernel, tb=tb),
        grid=(t // tb,),
        in_specs=[
            pl.BlockSpec((None, 2, tb), lambda i: (i, 0, 0), memory_space=pltpu.SMEM),
            pl.BlockSpec(memory_space=pl.ANY),
            pl.BlockSpec((tb, d), lambda i: (i, 0)),
            pl.BlockSpec((tb, LANES), lambda i: (i, 0)),
            bvec(), vec(), bvec(), bvec(),
        ],
        out_specs=pl.BlockSpec((tb, d), lambda i: (i, 0)),
        out_shape=jax.ShapeDtypeStruct((t, d), F32),
        scratch_shapes=[pltpu.VMEM((tb, d), F32), pltpu.VMEM((tb, d), F32), pltpu.SemaphoreType.DMA(())],
        compiler_params=_params(),
        name="l1_moe_combine",
    )(dest, ys, x, rw, gate, g.reshape(1, d), sh, sc)


def _moe_plan(idx, t):
    tm = MOE_TILE
    n_tiles = (2 * t) // tm + N_EXPERTS
    e_flat = idx.reshape(-1)
    onehot = (e_flat[:, None] == jnp.arange(N_EXPERTS, dtype=jnp.int32)[None, :]).astype(jnp.int32)
    incl = jnp.cumsum(onehot, axis=0)
    counts = incl[-1]
    pos = jnp.sum((incl - onehot) * onehot, axis=1)
    tiles_per_e = (counts + tm - 1) // tm
    tile_end = jnp.cumsum(tiles_per_e)
    row_start = (tile_end - tiles_per_e) * tm
    dest = row_start[e_flat] + pos
    n_used = tile_end[-1:]
    tile_ids = jnp.arange(n_tiles, dtype=jnp.int32)
    tile_expert = jnp.minimum(jnp.sum((tile_ids[:, None] >= tile_end[None, :]).astype(jnp.int32), axis=1),
                              N_EXPERTS - 1)
    last_e = jnp.max(jnp.where(counts > 0, jnp.arange(N_EXPERTS), 0))
    tile_expert = jnp.where(tile_ids < n_used[0], tile_expert, last_e).astype(jnp.int32)
    src = jnp.zeros((n_tiles * tm,), jnp.int32).at[dest].set(jnp.arange(2 * t, dtype=jnp.int32) // 2)
    tb = COMBINE_TILE
    dest_t = dest.reshape(t // tb, tb, 2).transpose(0, 2, 1)
    return src.reshape(n_tiles, 1, tm), dest_t, tile_expert, n_used.astype(jnp.int32)


def kernel(x, c, l0_ada_w, l0_ada_b, l0_norm1_g, l0_norm2_g, l0_ret_w_in, l0_ret_decay_logit, l0_ret_gn_g, l0_ret_w_out, l0_ffn_w_gate, l0_ffn_w_up, l0_ffn_w_down, l1_ada_w, l1_ada_b, l1_norm1_g, l1_norm2_g, l1_attn_w_qkv, l1_attn_q_norm_g, l1_attn_k_norm_g, l1_attn_w_out, l1_moe_w_router, l1_moe_b_router, l1_moe_w_gate, l1_moe_w_up, l1_moe_w_down, final_ada_w, final_ada_b, final_norm_g):
    bsz, seq, d = x.shape
    t = bsz * seq
    xf = x.reshape(t, d)

    c8 = jnp.zeros((8, d), F32).at[:bsz].set(c)

    def mods(w, b, n):
        m = _adaln(c8, w, b)[:bsz]
        return [m[:, i * d:(i + 1) * d].reshape(bsz, 1, d) for i in range(n)]

    sh1, sc1, g1, sh2, sc2, g2 = mods(l0_ada_w, l0_ada_b, 6)
    sh3, sc3, g3, sh4, sc4, g4 = mods(l1_ada_w, l1_ada_b, 6)
    fsh, fsc = mods(final_ada_w, final_ada_b, 2)

    bf = lambda w: w.astype(BF16)

    proj = _inproj0(xf, l0_norm1_g, sh1, sc1, bf(l0_ret_w_in), seq)
    log_gamma = jax.nn.log_sigmoid(l0_ret_decay_logit.astype(F32))
    u = _retention(proj, log_gamma, l0_ret_gn_g, bsz, seq)
    x1, h2 = _outproj(u, bf(l0_ret_w_out), xf, g1, l0_norm2_g, sh2, sc2, seq)
    x2, h3 = _ffn0(h2, bf(l0_ffn_w_gate), bf(l0_ffn_w_up), bf(l0_ffn_w_down), x1, g2, l1_norm1_g, sh3, sc3, seq)

    qkv = _qkv1(h3, bf(l1_attn_w_qkv), l1_attn_q_norm_g, l1_attn_k_norm_g, seq)
    o = _attention(qkv, bsz, seq)
    wr = jnp.zeros((d, LANES), F32).at[:, :N_EXPERTS].set(l1_moe_w_router)
    br = jnp.full((1, LANES), -jnp.inf, F32).at[0, :N_EXPERTS].set(l1_moe_b_router.astype(F32))
    x3, h4, idx, rw = _outproj(o, bf(l1_attn_w_out), x2, g3, l1_norm2_g, sh4, sc4, seq,
                               router=(wr, br), h_dtype=F32)
    src, dest, tile_expert, n_used = _moe_plan(idx[:, :2], t)
    ys = _moe_ffn(h4, src, tile_expert, n_used, bf(l1_moe_w_gate), bf(l1_moe_w_up), bf(l1_moe_w_down))
    out = _combine(dest, ys, x3, rw, g4, final_norm_g, fsh, fsc, seq)
    return out.reshape(bsz, seq, d)
```

```python
import functools

import numpy as np
import jax
import jax.numpy as jnp
from jax import lax
from jax.experimental import pallas as pl
from jax.experimental.pallas import tpu as pltpu

GRID_W = 64
NORM_EPS = 1e-6
ROPE_THETA = 10000.0
RET_HEADS = 4
RET_DK = 256
RET_DV = 512
RET_CHUNK = 128
ATT_HEADS = 8
ATT_KV_HEADS = 2
ATT_HD = 128
ATT_GROUP = ATT_HEADS // ATT_KV_HEADS
N_EXPERTS = 8
LANES = 128
LOG2_E = 1.4426950408889634

F32 = jnp.float32
BF16 = jnp.bfloat16
HIGHEST = lax.Precision.HIGHEST

TOKEN_TILE = 512
RET_BLOCK = 512
ATT_Q_TILE = 256
ATT_K_TILE = 1024
MOE_TILE = 512
MOE_FF_TILE = 1792
DISPATCH_TILE = 512
COMBINE_TILE = 256
VMEM_LIMIT = 56 * 1024 * 1024


def _params(**kw):
    return pltpu.CompilerParams(vmem_limit_bytes=VMEM_LIMIT, **kw)


def _resident(shape):
    return pl.BlockSpec(shape, lambda *_: (0,) * len(shape), pipeline_mode=pl.Buffered(1))


def _silu(a):
    return a * jax.nn.sigmoid(a)


def _rms_mod(x, g, shift, scale):
    y = x * lax.rsqrt(jnp.mean(x * x, axis=-1, keepdims=True) + NORM_EPS)
    return (y * g) * (1.0 + scale) + shift


def _adaln_kernel(c_ref, w_ref, b_ref, o_ref):
    a = _silu(c_ref[...])
    o_ref[...] = jnp.dot(a, w_ref[...], precision=HIGHEST, preferred_element_type=F32) + b_ref[...]


def _adaln(c8, w, b):
    d, n = w.shape
    tn = 1024
    return pl.pallas_call(
        _adaln_kernel,
        grid=(n // tn,),
        in_specs=[
            pl.BlockSpec((8, d), lambda j: (0, 0)),
            pl.BlockSpec((d, tn), lambda j: (0, j)),
            pl.BlockSpec((1, tn), lambda j: (0, j)),
        ],
        out_specs=pl.BlockSpec((8, tn), lambda j: (0, j)),
        out_shape=jax.ShapeDtypeStruct((8, n), F32),
        compiler_params=_params(),
        name="adaln",
    )(c8, w, b.reshape(1, n))


def _rope_tables(seq, half):
    nfreq = half // 2
    freqs = ROPE_THETA ** (-np.arange(0, nfreq, dtype=np.float64) * 2.0 / half)
    npos = max(seq // GRID_W, GRID_W)
    ang = np.arange(npos, dtype=np.float64)[:, None] * freqs[None, :]
    cos = np.concatenate([np.cos(ang), np.cos(ang)], axis=1)
    sin = np.concatenate([-np.sin(ang), np.sin(ang)], axis=1)
    return cos.astype(np.float32), sin.astype(np.float32)


def _inproj0_kernel(x_ref, g_ref, sh_ref, sc_ref, w_ref, cr_ref, sr_ref, cc_ref, scol_ref, o_ref, *, tm):
    h = _rms_mod(x_ref[...], g_ref[...], sh_ref[...], sc_ref[...]).astype(BF16)
    rows = tm // GRID_W
    cr = cr_ref[...]
    sr = sr_ref[...]
    cc = cc_ref[...][None]
    scol = scol_ref[...][None]
    nw = 512
    n_qk = 2 * RET_HEADS * RET_DK
    for c in range(o_ref.shape[1] // nw):
        y = jnp.dot(h, w_ref[:, c * nw:(c + 1) * nw], preferred_element_type=F32)
        if c * nw < n_qk:
            is_k = c * nw >= RET_HEADS * RET_DK
            for s in range(nw // LANES):
                ys = y[:, s * LANES:(s + 1) * LANES]
                y3 = ys.reshape(rows, GRID_W, LANES)
                r3 = pltpu.roll(ys, LANES // 2, 1).reshape(rows, GRID_W, LANES)
                if s % 2 == 0:
                    o3 = y3 * cr + r3 * sr
                else:
                    o3 = y3 * cc + r3 * scol
                if is_k:
                    o3 = o3 * (RET_DK ** -0.5)
                o_ref[:, c * nw + s * LANES:c * nw + (s + 1) * LANES] = o3.reshape(tm, LANES).astype(BF16)
        else:
            o_ref[:, c * nw:(c + 1) * nw] = y.astype(BF16)


def _inproj0(x, g, sh, sc, w_bf16, seq):
    t, d = x.shape
    n = w_bf16.shape[1]
    tm = TOKEN_TILE
    per_b = seq // tm
    rows = tm // GRID_W
    cos, sin = _rope_tables(seq, LANES)
    n_img_rows = seq // GRID_W
    cr = jnp.asarray(cos[:n_img_rows].reshape(n_img_rows, 1, LANES))
    sr = jnp.asarray(sin[:n_img_rows].reshape(n_img_rows, 1, LANES))
    cc = jnp.asarray(cos[:GRID_W])
    scol = jnp.asarray(sin[:GRID_W])
    vec = lambda: pl.BlockSpec((1, d), lambda i: (0, 0))
    bvec = lambda: pl.BlockSpec((None, 1, d), lambda i: (i // per_b, 0, 0))
    rowt = lambda: pl.BlockSpec((rows, 1, LANES), lambda i: (i % per_b, 0, 0))
    colt = lambda: pl.BlockSpec((GRID_W, LANES), lambda i: (0, 0))
    return pl.pallas_call(
        functools.partial(_inproj0_kernel, tm=tm),
        grid=(t // tm,),
        in_specs=[pl.BlockSpec((tm, d), lambda i: (i, 0)), vec(), bvec(), bvec(),
                  _resident((d, n)), rowt(), rowt(), colt(), colt()],
        out_specs=pl.BlockSpec((tm, n), lambda i: (i, 0)),
        out_shape=jax.ShapeDtypeStruct((t, n), BF16),
        compiler_params=_params(),
        name="l0_inproj",
    )(x, g.reshape(1, d), sh, sc, w_bf16, cr, sr, cc, scol)


def _col(n):
    return lax.broadcasted_iota(jnp.int32, (n, 1), 0).astype(F32)


def _ret_fwd_kernel(lg_ref, q_ref, k_ref, v_ref, y_ref, state_ref, *, nchunks):
    hd = pl.program_id(1)

    @pl.when(pl.program_id(2) == 0)
    def _():
        state_ref[...] = jnp.zeros_like(state_ref)

    c_len = RET_CHUNK
    lgf = lg_ref[0, hd]
    lgb = lg_ref[1, hd]
    ii = lax.broadcasted_iota(jnp.int32, (c_len, c_len), 0)
    jj = lax.broadcasted_iota(jnp.int32, (c_len, c_len), 1)
    diff = (ii - jj).astype(F32)
    dmat = jnp.where(diff >= 0, jnp.exp(diff * lgf), jnp.exp(-diff * lgb))
    idx = _col(c_len)
    xi = jnp.exp((idx + 1.0) * lgf)
    zeta = jnp.exp((c_len - 1.0 - idx) * lgf)
    decay = jnp.exp(jnp.full((1, 1), c_len, F32) * lgf)
    for c in range(nchunks):
        sl = slice(c * c_len, (c + 1) * c_len)
        q = q_ref[sl, :]
        k = k_ref[sl, :]
        v = v_ref[sl, :]
        s = lax.dot_general(q, k, (((1,), (1,)), ((), ())), preferred_element_type=F32) * dmat
        y = jnp.dot(s.astype(BF16), v, preferred_element_type=F32)
        st = state_ref[...]
        y = y + jnp.dot(q, st.astype(BF16), preferred_element_type=F32) * xi
        kz = (k.astype(F32) * zeta).astype(BF16)
        upd = lax.dot_general(kz, v, (((0,), (0,)), ((), ())), preferred_element_type=F32)
        state_ref[...] = st * decay + upd
        y_ref[sl, :] = y.astype(y_ref.dtype)


def _ret_bwd_kernel(lg_ref, q_ref, k_ref, v_ref, g_ref, yf_ref, gn_ref, u_ref, state_ref, *, nchunks):
    hd = pl.program_id(1)

    @pl.when(pl.program_id(2) == 0)
    def _():
        state_ref[...] = jnp.zeros_like(state_ref)

    c_len = RET_CHUNK
    lgb = lg_ref[1, hd]
    idx = _col(c_len)
    xi = jnp.exp((c_len - idx) * lgb)
    zeta = jnp.exp(idx * lgb)
    decay = jnp.exp(jnp.full((1, 1), c_len, F32) * lgb)
    gn = gn_ref[...]
    for c in reversed(range(nchunks)):
        sl = slice(c * c_len, (c + 1) * c_len)
        q = q_ref[sl, :]
        k = k_ref[sl, :]
        v = v_ref[sl, :]
        st = state_ref[...]
        y = yf_ref[sl, :].astype(F32) + jnp.dot(q, st.astype(BF16), preferred_element_type=F32) * xi
        kz = (k.astype(F32) * zeta).astype(BF16)
        upd = lax.dot_general(kz, v, (((0,), (0,)), ((), ())), preferred_element_type=F32)
        state_ref[...] = st * decay + upd
        mu = jnp.mean(y, axis=-1, keepdims=True)
        yc = y - mu
        var = jnp.mean(yc * yc, axis=-1, keepdims=True)
        yn = yc * lax.rsqrt(var + NORM_EPS) * gn
        u_ref[sl, :] = (_silu(g_ref[sl, :].astype(F32)) * yn).astype(u_ref.dtype)


def _retention(proj, log_gamma, gn_g, bsz, seq):
    t = proj.shape[0]
    tb = RET_BLOCK
    nb = seq // tb
    nchunks = tb // RET_CHUNK
    hk = RET_HEADS
    grid = (bsz, RET_HEADS, nb)
    qspec = lambda f: pl.BlockSpec((tb, RET_DK), lambda b, h, n, lg: (b * nb + f(n), h))
    kspec = lambda f: pl.BlockSpec((tb, RET_DK), lambda b, h, n, lg: (b * nb + f(n), hk + h))
    vspec = lambda f: pl.BlockSpec((tb, RET_DV), lambda b, h, n, lg: (b * nb + f(n), hk + h))
    gspec = lambda f: pl.BlockSpec((tb, RET_DV), lambda b, h, n, lg: (b * nb + f(n), 2 * hk + h))
    yspec = lambda f: pl.BlockSpec((tb, RET_DV), lambda b, h, n, lg: (b * nb + f(n), h))
    fwd = lambda n: n
    rev = lambda n: nb - 1 - n
    state = pltpu.VMEM((RET_DK, RET_DV), F32)
    yf = pl.pallas_call(
        functools.partial(_ret_fwd_kernel, nchunks=nchunks),
        grid_spec=pltpu.PrefetchScalarGridSpec(
            num_scalar_prefetch=1, grid=grid,
            in_specs=[qspec(fwd), kspec(fwd), vspec(fwd)],
            out_specs=yspec(fwd), scratch_shapes=[state]),
        out_shape=jax.ShapeDtypeStruct((t, RET_HEADS * RET_DV), BF16),
        compiler_params=_params(),
        name="l0_ret_fwd",
    )(log_gamma, proj, proj, proj)
    return pl.pallas_call(
        functools.partial(_ret_bwd_kernel, nchunks=nchunks),
        grid_spec=pltpu.PrefetchScalarGridSpec(
            num_scalar_prefetch=1, grid=grid,
            in_specs=[qspec(rev), kspec(rev), vspec(rev), gspec(rev), yspec(rev),
                      pl.BlockSpec((1, RET_DV), lambda b, h, n, lg: (0, h))],
            out_specs=yspec(rev), scratch_shapes=[state]),
        out_shape=jax.ShapeDtypeStruct((t, RET_HEADS * RET_DV), BF16),
        compiler_params=_params(),
        name="l0_ret_bwd",
    )(log_gamma, proj, proj, proj, proj, yf, gn_g.reshape(1, -1))


def _top2(logits):
    lane = lax.broadcasted_iota(jnp.int32, logits.shape, 1)
    v1 = jnp.max(logits, axis=-1, keepdims=True)
    i1 = jnp.min(jnp.where(logits == v1, lane, LANES), axis=-1, keepdims=True)
    rest = jnp.where(lane == i1, -jnp.inf, logits)
    v2 = jnp.max(rest, axis=-1, keepdims=True)
    i2 = jnp.min(jnp.where(rest == v2, lane, LANES), axis=-1, keepdims=True)
    e = jnp.exp(v2 - v1)
    w1 = 1.0 / (1.0 + e)
    w2 = e / (1.0 + e)
    idx = jnp.where(lane == 0, i1, jnp.where(lane == 1, i2, 0))
    w = jnp.where(lane == 0, w1, jnp.where(lane == 1, w2, 0.0))
    return idx, w


def _outproj_kernel(u_ref, w_ref, x_ref, gate_ref, g_ref, sh_ref, sc_ref, *rest, router):
    if router:
        wr_ref, br_ref, xo_ref, ho_ref, idx_ref, rw_ref = rest
    else:
        xo_ref, ho_ref = rest
    m = jnp.dot(u_ref[...], w_ref[...], preferred_element_type=F32)
    x1 = x_ref[...] + gate_ref[...] * m
    xo_ref[...] = x1
    h = _rms_mod(x1, g_ref[...], sh_ref[...], sc_ref[...])
    ho_ref[...] = h.astype(ho_ref.dtype)
    if router:
        logits = jnp.dot(h, wr_ref[...], precision=HIGHEST, preferred_element_type=F32) + br_ref[...]
        idx, w = _top2(logits)
        idx_ref[...] = idx
        rw_ref[...] = w


def _outproj(u, w_bf16, x, gate, g, sh, sc, seq, router=None, h_dtype=BF16):
    t, d = x.shape
    kdim = u.shape[1]
    tm = TOKEN_TILE
    per_b = seq // tm
    row = lambda width: pl.BlockSpec((tm, width), lambda i: (i, 0))
    vec = lambda: pl.BlockSpec((1, d), lambda i: (0, 0))
    bvec = lambda: pl.BlockSpec((None, 1, d), lambda i: (i // per_b, 0, 0))
    in_specs = [row(kdim), _resident((kdim, d)), row(d), bvec(), vec(), bvec(), bvec()]
    args = [u, w_bf16, x, gate, g.reshape(1, d), sh, sc]
    out_specs = [row(d), row(d)]
    out_shape = [jax.ShapeDtypeStruct((t, d), F32), jax.ShapeDtypeStruct((t, d), h_dtype)]
    if router is not None:
        wr, br = router
        in_specs += [_resident((d, LANES)), pl.BlockSpec((1, LANES), lambda i: (0, 0))]
        args += [wr, br]
        out_specs += [row(LANES), row(LANES)]
        out_shape += [jax.ShapeDtypeStruct((t, LANES), jnp.int32), jax.ShapeDtypeStruct((t, LANES), F32)]
    return pl.pallas_call(
        functools.partial(_outproj_kernel, router=router is not None),
        grid=(t // tm,),
        in_specs=in_specs, out_specs=out_specs, out_shape=out_shape,
        compiler_params=_params(),
        name="outproj_router" if router is not None else "outproj",
    )(*args)


def _ffn0_kernel(h_ref, wg_ref, wu_ref, wd_ref, x_ref, gate_ref, g_ref, sh_ref, sc_ref, xo_ref, ho_ref, *, tf):
    h = h_ref[...]
    acc = jnp.zeros(x_ref.shape, F32)
    for c in range(wg_ref.shape[1] // tf):
        cs = slice(c * tf, (c + 1) * tf)
        a = jnp.dot(h, wg_ref[:, cs], preferred_element_type=F32)
        b = jnp.dot(h, wu_ref[:, cs], preferred_element_type=F32)
        act = (_silu(a) * b).astype(BF16)
        acc = acc + jnp.dot(act, wd_ref[cs, :], preferred_element_type=F32)
    x2 = x_ref[...] + gate_ref[...] * acc
    xo_ref[...] = x2
    ho_ref[...] = _rms_mod(x2, g_ref[...], sh_ref[...], sc_ref[...]).astype(ho_ref.dtype)


def _ffn0(h, wg, wu, wd, x, gate, g, sh, sc, seq):
    t, d = x.shape
    f = wg.shape[1]
    tm = TOKEN_TILE
    per_b = seq // tm
    row = lambda: pl.BlockSpec((tm, d), lambda i: (i, 0))
    vec = lambda: pl.BlockSpec((1, d), lambda i: (0, 0))
    bvec = lambda: pl.BlockSpec((None, 1, d), lambda i: (i // per_b, 0, 0))
    return pl.pallas_call(
        functools.partial(_ffn0_kernel, tf=f // 2),
        grid=(t // tm,),
        in_specs=[row(), _resident((d, f)), _resident((d, f)), _resident((f, d)), row(), bvec(), vec(), bvec(), bvec()],
        out_specs=[row(), row()],
        out_shape=[jax.ShapeDtypeStruct((t, d), F32), jax.ShapeDtypeStruct((t, d), BF16)],
        compiler_params=_params(),
        name="l0_ffn",
    )(h, wg, wu, wd, x, gate, g.reshape(1, d), sh, sc)


def _qkv1_kernel(h_ref, w_ref, qg_ref, kg_ref, cr_ref, s1r_ref, s2r_ref, cc_ref, s1c_ref, s2c_ref, o_ref, *, tm):
    rows = tm // GRID_W
    y = jnp.dot(h_ref[...], w_ref[...], preferred_element_type=F32)
    cos =(cr_ref[...] + cc_ref[...][None]).reshape(tm, LANES)
    s1 = (s1r_ref[...] + s1c_ref[...][None]).reshape(tm, LANES)
    s2 = (s2r_ref[...] + s2c_ref[...][None]).reshape(tm, LANES)
    n_rot = ATT_HEADS + ATT_KV_HEADS
    for j in range(n_rot):
        ys = y[:, j * LANES:(j + 1) * LANES]
        gain = qg_ref[...] if j < ATT_HEADS else kg_ref[...]
        yn = ys * lax.rsqrt(jnp.mean(ys * ys, axis=-1, keepdims=True) + NORM_EPS) * gain
        o = yn * cos + pltpu.roll(yn, 96, 1) * s1 + pltpu.roll(yn, 32, 1) * s2
        if j < ATT_HEADS:
            o = o * (ATT_HD ** -0.5 * LOG2_E)
        o_ref[:, j * LANES:(j + 1) * LANES] = o.astype(BF16)
    o_ref[:, n_rot * LANES:] = y[:, n_rot * LANES:].astype(BF16)


def _att_rope_tables(seq):
    half = ATT_HD // 2
    nfreq = half // 2
    freqs = ROPE_THETA ** (-np.arange(0, nfreq, dtype=np.float64) * 2.0 / half)
    lane = np.arange(LANES)
    f_l = freqs[lane % nfreq]
    first = (lane % half) < nfreq
    is_row = lane < half
    n_img_rows = seq // GRID_W

    def tables(npos, mask):
        ang = np.arange(npos, dtype=np.float64)[:, None] * f_l[None, :]
        c = np.where(mask[None], np.cos(ang), 0.0)
        s1 = np.where((mask & first)[None], -np.sin(ang), 0.0)
        s2 = np.where((mask & ~first)[None], np.sin(ang), 0.0)
        return [a.astype(np.float32) for a in (c, s1, s2)]

    row_t = [jnp.asarray(a.reshape(n_img_rows, 1, LANES)) for a in tables(n_img_rows, is_row)]
    col_t = [jnp.asarray(a) for a in tables(GRID_W, ~is_row)]
    return row_t, col_t


def _qkv1(h, w_bf16, qg, kg, seq):
    t, d = h.shape
    n = w_bf16.shape[1]
    tm = TOKEN_TILE
    per_b = seq // tm
    rows = tm // GRID_W
    row_t, col_t = _att_rope_tables(seq)
    rowt = lambda: pl.BlockSpec((rows, 1, LANES), lambda i: (i % per_b, 0, 0))
    colt = lambda: pl.BlockSpec((GRID_W, LANES), lambda i: (0, 0))
    hvec = lambda: pl.BlockSpec((1, LANES), lambda i: (0, 0))
    return pl.pallas_call(
        functools.partial(_qkv1_kernel, tm=tm),
        grid=(t // tm,),
        in_specs=[pl.BlockSpec((tm, d), lambda i: (i, 0)), _resident((d, n)), hvec(), hvec(),
                  rowt(), rowt(), rowt(), colt(), colt(), colt()],
        out_specs=pl.BlockSpec((tm, n), lambda i: (i, 0)),
        out_shape=jax.ShapeDtypeStruct((t, n), BF16),
        compiler_params=_params(),
        name="l1_qkv",
    )(h, w_bf16, qg.reshape(1, LANES), kg.reshape(1, LANES), *row_t, *col_t)


def _flash_kernel(q_ref, k_ref, vt_ref, o_ref, qs_ref, m_ref, l_ref, acc_ref, s_ref, *, tq, tk):
    for g in range(ATT_GROUP):
        qs_ref[g * tq:(g + 1) * tq, :] = q_ref[:, g * LANES:(g + 1) * LANES]
    m_ref[...] = jnp.full(m_ref.shape, -jnp.inf, F32)
    l_ref[...] = jnp.zeros_like(l_ref)
    acc_ref[...] = jnp.zeros_like(acc_ref)
    nk = vt_ref.shape[0]

    def scores(j, slot):
        k_c = k_ref[pl.ds(pl.multiple_of(j * tk, tk), tk), :]
        s_ref[slot] = lax.dot_general(k_c, qs_ref[...], (((1,), (1,)), ((), ())), preferred_element_type=F32)

    def update(j, slot):
        s = s_ref[slot]
        m_old = m_ref[...]
        m_new = jnp.maximum(m_old, jnp.max(s, axis=0, keepdims=True))
        p = jnp.exp2(s - m_new)
        alpha = jnp.exp2(m_old - m_new)
        l_ref[...] = alpha * l_ref[...] + jnp.sum(p, axis=0, keepdims=True)
        acc_ref[...] = alpha * acc_ref[...] + jnp.dot(vt_ref[j], p.astype(BF16), preferred_element_type=F32)
        m_ref[...] = m_new

    scores(0, 0)

    def body(jj, carry):
        j = 2 * jj
        scores(j + 1, 1)
        update(j, 0)
        scores(jnp.minimum(j + 2, nk - 1), 0)
        update(j + 1, 1)
        return carry

    lax.fori_loop(0, nk // 2, body, 0)
    o = acc_ref[...] / l_ref[...]
    for g in range(ATT_GROUP):
        o_ref[:, g * LANES:(g + 1) * LANES] = o[:, g * tq:(g + 1) * tq].T.astype(o_ref.dtype)


def _attention(qkv, bsz, seq):
    t = qkv.shape[0]
    tq = ATT_Q_TILE
    tk = min(ATT_K_TILE, seq)
    nq = seq // tq
    nk = seq // tk
    assert nk % 2 == 0 and nk * tk == seq, "the score pipeline consumes key chunks in pairs"
    gw = ATT_GROUP * ATT_HD
    nqs = ATT_GROUP * tq
    v = qkv[:, (ATT_HEADS + ATT_KV_HEADS) * ATT_HD:]
    vt = v.reshape(bsz, nk, tk, ATT_KV_HEADS, ATT_HD).transpose(0, 3, 1, 4, 2)
    return pl.pallas_call(
        functools.partial(_flash_kernel, tq=tq, tk=tk),
        grid=(bsz, ATT_KV_HEADS, nq),
        in_specs=[
            pl.BlockSpec((tq, gw), lambda b, kv, i: (b * nq + i, kv)),
            pl.BlockSpec((seq, ATT_HD), lambda b, kv, i: (b, ATT_HEADS + kv)),
            pl.BlockSpec((None, None, nk, ATT_HD, tk), lambda b, kv, i: (b, kv, 0, 0, 0)),
        ],
        out_specs=pl.BlockSpec((tq, gw), lambda b, kv, i: (b * nq + i, kv)),
        out_shape=jax.ShapeDtypeStruct((t, ATT_HEADS * ATT_HD), BF16),
        scratch_shapes=[pltpu.VMEM((nqs, ATT_HD), BF16), pltpu.VMEM((1, nqs), F32), pltpu.VMEM((1, nqs), F32),
                        pltpu.VMEM((ATT_HD, nqs), F32), pltpu.VMEM((2, tk, nqs), F32)],
        compiler_params=_params(),
        name="l1_attention",
    )(qkv, qkv, vt)


def _dispatch_kernel(lt_ref, dest_ref, h_ref, xs_hbm, zero_ref, zsem, sem, *, tb, tm):
    @pl.when(pl.program_id(0) == 0)
    def _():
        zero_ref[...] = jnp.zeros_like(zero_ref)
        for e in range(2 * N_EXPERTS):
            fill = pltpu.make_async_copy(zero_ref, xs_hbm.at[pl.ds(lt_ref[e] * tm, tm), :], zsem)
            fill.start()
            fill.wait()

    for r in range(tb):
        for k in range(2):
            pltpu.make_async_copy(h_ref.at[pl.ds(r, 1), :], xs_hbm.at[pl.ds(dest_ref[k, r], 1), :], sem).start()
    for k in range(2):
        pltpu.make_async_copy(h_ref, xs_hbm.at[pl.ds(0, tb), :], sem).wait()


def _dispatch(h, dest, last_tile, n_tiles):
    t, d = h.shape
    tb = DISPATCH_TILE
    tm = MOE_TILE
    return pl.pallas_call(
        functools.partial(_dispatch_kernel, tb=tb, tm=tm),
        grid_spec=pltpu.PrefetchScalarGridSpec(
            num_scalar_prefetch=1, grid=(t // tb,),
            in_specs=[
                pl.BlockSpec((None, 2, tb), lambda i, lt: (i, 0, 0), memory_space=pltpu.SMEM),
                pl.BlockSpec((tb, d), lambda i, lt: (i, 0)),
            ],
            out_specs=pl.BlockSpec(memory_space=pl.ANY),
            scratch_shapes=[pltpu.VMEM((tm, d), F32), pltpu.SemaphoreType.DMA(()), pltpu.SemaphoreType.DMA(())]),
        out_shape=jax.ShapeDtypeStruct((n_tiles * tm, d), F32),
        compiler_params=_params(),
        name="l1_moe_dispatch",
    )(last_tile, dest, h)


def _moe_ffn_kernel(te_ref, nu_ref, x_ref, wg_ref, wu_ref, wd_ref, y_ref, xb_ref, acc_ref):
    i = pl.program_id(0)
    j = pl.program_id(1)
    nj = pl.num_programs(1)
    used = i < nu_ref[0]

    @pl.when(jnp.logical_and(used, j == 0))
    def _():
        xb_ref[...] = x_ref[...].astype(BF16)

    @pl.when(used)
    def _():
        xb = xb_ref[...]
        a = jnp.dot(xb, wg_ref[...], preferred_element_type=F32)
        b = jnp.dot(xb, wu_ref[...], preferred_element_type=F32)
        act = (_silu(a) * b).astype(BF16)
        part = jnp.dot(act, wd_ref[...], preferred_element_type=F32)

        @pl.when(j == 0)
        def _():
            acc_ref[...] = part

        @pl.when(jnp.logical_and(j > 0, j < nj - 1))
        def _():
            acc_ref[...] += part

        @pl.when(j == nj - 1)
        def _():
            y_ref[...] = acc_ref[...] + part

    @pl.when(jnp.logical_and(jnp.logical_not(used), j == nj - 1))
    def _():
        y_ref[...] = jnp.zeros_like(y_ref)


def _moe_ffn(xs, tile_expert, n_used, wg, wu, wd):
    d = xs.shape[1]
    tm = MOE_TILE
    n_tiles = xs.shape[0] // tm
    tf = MOE_FF_TILE
    f = wg.shape[2]
    nj = f // tf
    assert nj >= 2 and nj * tf == f

    def jeff(i, j, nu):
        return jnp.where(i < nu[0], j, nj - 1)

    return pl.pallas_call(
        _moe_ffn_kernel,
        grid_spec=pltpu.PrefetchScalarGridSpec(
            num_scalar_prefetch=2, grid=(n_tiles, nj),
            in_specs=[
                pl.BlockSpec((tm, d), lambda i, j, te, nu: (jnp.minimum(i, nu[0] - 1), 0)),
                pl.BlockSpec((None, d, tf), lambda i, j, te, nu: (te[i], 0, jeff(i, j, nu))),
                pl.BlockSpec((None, d, tf), lambda i, j, te, nu: (te[i], 0, jeff(i, j, nu))),
                pl.BlockSpec((None, tf, d), lambda i, j, te, nu: (te[i], jeff(i, j, nu), 0)),
            ],
            out_specs=pl.BlockSpec((tm, d), lambda i, j, te, nu: (i, 0)),
            scratch_shapes=[pltpu.VMEM((tm, d), BF16), pltpu.VMEM((tm, d), F32)]),
        out_shape=jax.ShapeDtypeStruct((n_tiles * tm, d), F32),
        compiler_params=_params(),
        name="l1_moe_ffn",
    )(tile_expert, n_used, xs, wg, wu, wd)


def _combine_kernel(dfirst_ref, dnext_ref, y_hbm, x_ref, rw_ref, gate_ref, g_ref, sh_ref, sc_ref, o_ref,
                    ya_ref, yb_ref, sem, *, tb):
    i = pl.program_id(0)
    n = pl.num_programs(0)
    slot = i % 2

    def row_copies(dest_ref, r, buf_slot):
        ca = pltpu.make_async_copy(y_hbm.at[pl.ds(dest_ref[0, r], 1), :], ya_ref.at[buf_slot, pl.ds(r, 1), :],
                                   sem.at[buf_slot])
        cb = pltpu.make_async_copy(y_hbm.at[pl.ds(dest_ref[1, r], 1), :], yb_ref.at[buf_slot, pl.ds(r, 1), :],
                                   sem.at[buf_slot])
        return ca, cb

    @pl.when(i == 0)
    def _():
        def issue(r, c):
            ca, cb = row_copies(dfirst_ref, r, 0)
            ca.start()
            cb.start()
            return c
        lax.fori_loop(0, tb, issue, 0)

    @pl.when(i + 1 < n)
    def _():
        for r in range(tb):
            ca, cb = row_copies(dnext_ref, r, 1 - slot)
            ca.start()
            cb.start()

    pltpu.make_async_copy(y_hbm.at[pl.ds(0, tb), :], ya_ref.at[slot], sem.at[slot]).wait()
    pltpu.make_async_copy(y_hbm.at[pl.ds(0, tb), :], yb_ref.at[slot], sem.at[slot]).wait()
    rw = rw_ref[...]
    f = rw[:, 0:1] * ya_ref[slot] + rw[:, 1:2] * yb_ref[slot]
    x = x_ref[...] + gate_ref[...] * f
    o_ref[...] = _rms_mod(x, g_ref[...], sh_ref[...], sc_ref[...])


def _combine(dest, ys, x, rw, gate, g, sh, sc, seq):
    t, d = x.shape
    tb = COMBINE_TILE
    per_b = seq // tb
    n = t // tb
    vec = lambda: pl.BlockSpec((1, d), lambda i: (0, 0))
    bvec = lambda: pl.BlockSpec((None, 1, d), lambda i: (i // per_b, 0, 0))
    return pl.pallas_call(
        functools.partial(_combine_kernel, tb=tb),
        grid=(n,),
        in_specs=[
            pl.BlockSpec((None, 2, tb), lambda i: (0, 0, 0), memory_space=pltpu.SMEM),
            pl.BlockSpec((None, 2, tb), lambda i: (jnp.minimum(i + 1, n - 1), 0, 0), memory_space=pltpu.SMEM),
            pl.BlockSpec(memory_space=pl.ANY),
            pl.BlockSpec((tb, d), lambda i: (i, 0)),
            pl.BlockSpec((tb, LANES), lambda i: (i, 0)),
            bvec(), vec(), bvec(), bvec(),
        ],
        out_specs=pl.BlockSpec((tb, d), lambda i: (i, 0)),
        out_shape=jax.ShapeDtypeStruct((t, d), F32),
        scratch_shapes=[pltpu.VMEM((2, tb, d), F32), pltpu.VMEM((2, tb, d), F32), pltpu.SemaphoreType.DMA((2,))],
        compiler_params=_params(),
        name="l1_moe_combine",
    )(dest, dest, ys, x, rw, gate, g.reshape(1, d), sh, sc)


def _moe_plan(idx, t):
    tm = MOE_TILE
    n_tiles = (2 * t) // tm + N_EXPERTS
    e_flat = idx.reshape(-1)
    experts = jnp.arange(N_EXPERTS, dtype=jnp.int32)
    onehot = (e_flat[:, None] == experts[None, :]).astype(jnp.int32)
    incl = jnp.cumsum(onehot, axis=0)
    counts = incl[-1]
    pos = jnp.sum((incl - onehot) * onehot, axis=1)
    tiles_per_e = (counts + tm - 1) // tm
    tile_end = jnp.cumsum(tiles_per_e)
    row_start = (tile_end - tiles_per_e) * tm
    dest = (row_start[e_flat] + pos).astype(jnp.int32)
    n_used = tile_end[-1:]
    tile_ids = jnp.arange(n_tiles, dtype=jnp.int32)
    tile_expert = jnp.minimum(jnp.sum((tile_ids[:, None] >= tile_end[None, :]).astype(jnp.int32), axis=1),
                              N_EXPERTS - 1)
    last_e = jnp.max(jnp.where(counts > 0, experts, 0))
    tile_expert = jnp.where(tile_ids < n_used[0], tile_expert, last_e).astype(jnp.int32)
    last_tile = jnp.maximum(tile_end - 1, 0)
    unused = jnp.minimum(n_used[0] + experts, n_tiles - 1)
    clear_tiles = jnp.concatenate([last_tile, unused]).astype(jnp.int32)
    return dest, tile_expert, n_used.astype(jnp.int32), clear_tiles, n_tiles


def _by_tile(dest, t, tb):
    return dest.reshape(t // tb, tb, 2).transpose(0, 2, 1)


def kernel(x, c, l0_ada_w, l0_ada_b, l0_norm1_g, l0_norm2_g, l0_ret_w_in, l0_ret_decay_logit, l0_ret_gn_g, l0_ret_w_out, l0_ffn_w_gate, l0_ffn_w_up, l0_ffn_w_down, l1_ada_w, l1_ada_b, l1_norm1_g, l1_norm2_g, l1_attn_w_qkv, l1_attn_q_norm_g, l1_attn_k_norm_g, l1_attn_w_out, l1_moe_w_router, l1_moe_b_router, l1_moe_w_gate, l1_moe_w_up, l1_moe_w_down, final_ada_w, final_ada_b, final_norm_g):
    bsz, seq, d = x.shape
    t = bsz * seq
    xf = x.reshape(t, d)

    c8 = jnp.zeros((8, d), F32).at[:bsz].set(c)

    def mods(w, b, n):
        m = _adaln(c8, w, b)[:bsz]
        return [m[:, i * d:(i + 1) * d].reshape(bsz, 1, d) for i in range(n)]

    sh1, sc1, g1, sh2, sc2, g2 = mods(l0_ada_w, l0_ada_b, 6)
    sh3, sc3, g3, sh4, sc4, g4 = mods(l1_ada_w, l1_ada_b, 6)
    fsh, fsc = mods(final_ada_w, final_ada_b, 2)

    bf = lambda w: w.astype(BF16)

    proj = _inproj0(xf, l0_norm1_g, sh1, sc1, bf(l0_ret_w_in), seq)
    log_gamma = jax.nn.log_sigmoid(l0_ret_decay_logit.astype(F32))
    u = _retention(proj, log_gamma, l0_ret_gn_g, bsz, seq)
    x1, h2 = _outproj(u, bf(l0_ret_w_out), xf, g1, l0_norm2_g, sh2, sc2, seq)
    x2, h3 = _ffn0(h2, bf(l0_ffn_w_gate), bf(l0_ffn_w_up), bf(l0_ffn_w_down), x1, g2, l1_norm1_g, sh3, sc3, seq)

    qkv = _qkv1(h3, bf(l1_attn_w_qkv), l1_attn_q_norm_g, l1_attn_k_norm_g, seq)
    o = _attention(qkv, bsz, seq)
    wr = jnp.zeros((d, LANES), F32).at[:, :N_EXPERTS].set(l1_moe_w_router)
    br = jnp.full((1, LANES), -jnp.inf, F32).at[0, :N_EXPERTS].set(l1_moe_b_router.astype(F32))
    x3, h4, idx, rw = _outproj(o, bf(l1_attn_w_out), x2, g3, l1_norm2_g, sh4, sc4, seq,
                               router=(wr, br), h_dtype=F32)
    dest, tile_expert, n_used, last_tile, n_tiles = _moe_plan(idx[:, :2], t)
    xs = _dispatch(h4, _by_tile(dest, t, DISPATCH_TILE), last_tile, n_tiles)
    ys = _moe_ffn(xs, tile_expert, n_used, bf(l1_moe_w_gate), bf(l1_moe_w_up), bf(l1_moe_w_down))
    out = _combine(_by_tile(dest, t, COMBINE_TILE), ys, x3, rw, g4, final_norm_g, fsh, fsc, seq)
    return out.reshape(bsz, seq, d)
```

```python
import functools

import numpy as np
import jax
import jax.numpy as jnp
from jax import lax
from jax.experimental import pallas as pl
from jax.experimental.pallas import tpu as pltpu

GRID_W = 64
NORM_EPS = 1e-6
ROPE_THETA = 10000.0
RET_HEADS = 4
RET_DK = 256
RET_DV = 512
RET_CHUNK = 128
ATT_HEADS = 8
ATT_KV_HEADS = 2
ATT_HD = 128
ATT_GROUP = ATT_HEADS // ATT_KV_HEADS
N_EXPERTS = 8
LANES = 128
LOG2_E = 1.4426950408889634

F32 = jnp.float32
BF16 = jnp.bfloat16
HIGHEST = lax.Precision.HIGHEST

TOKEN_TILE = 512
RET_BLOCK = 512
ATT_Q_TILE = 256
ATT_K_TILE = 1024
ATT_SLOTS = 4
MOE_TILE = 512
MOE_FF_TILE = 1792
DISPATCH_TILE = 512
COMBINE_TILE = 256
VMEM_LIMIT = 56 * 1024 * 1024


def _params(**kw):
    return pltpu.CompilerParams(vmem_limit_bytes=VMEM_LIMIT, **kw)


def _resident(shape):
    return pl.BlockSpec(shape, lambda *_: (0,) * len(shape), pipeline_mode=pl.Buffered(1))


def _silu(a):
    return a * jax.nn.sigmoid(a)


def _rms_mod(x, g, shift, scale):
    y = x * lax.rsqrt(jnp.mean(x * x, axis=-1, keepdims=True) + NORM_EPS)
    return (y * g) * (1.0 + scale) + shift


def _adaln_kernel(c_ref, w_ref, b_ref, o_ref):
    a = _silu(c_ref[...])
    o_ref[...] = jnp.dot(a, w_ref[...], precision=HIGHEST, preferred_element_type=F32) + b_ref[...]


def _adaln(c8, w, b):
    d, n = w.shape
    tn = 1024
    return pl.pallas_call(
        _adaln_kernel,
        grid=(n // tn,),
        in_specs=[
            pl.BlockSpec((8, d), lambda j: (0, 0)),
            pl.BlockSpec((d, tn), lambda j: (0, j)),
            pl.BlockSpec((1, tn), lambda j: (0, j)),
        ],
        out_specs=pl.BlockSpec((8, tn), lambda j: (0, j)),
        out_shape=jax.ShapeDtypeStruct((8, n), F32),
        compiler_params=_params(),
        name="adaln",
    )(c8, w, b.reshape(1, n))


def _rope_tables(seq, half):
    nfreq = half // 2
    freqs = ROPE_THETA ** (-np.arange(0, nfreq, dtype=np.float64) * 2.0 / half)
    npos = max(seq // GRID_W, GRID_W)
    ang = np.arange(npos, dtype=np.float64)[:, None] * freqs[None, :]
    cos = np.concatenate([np.cos(ang), np.cos(ang)], axis=1)
    sin = np.concatenate([-np.sin(ang), np.sin(ang)], axis=1)
    return cos.astype(np.float32), sin.astype(np.float32)


def _inproj0_kernel(x_ref, g_ref, sh_ref, sc_ref, w_ref, cr_ref, sr_ref, cc_ref, scol_ref, o_ref, *, tm):
    h = _rms_mod(x_ref[...], g_ref[...], sh_ref[...], sc_ref[...]).astype(BF16)
    rows = tm // GRID_W
    cr = cr_ref[...]
    sr = sr_ref[...]
    cc = cc_ref[...][None]
    scol = scol_ref[...][None]
    nw = 512
    n_qk = 2 * RET_HEADS * RET_DK
    for c in range(o_ref.shape[1] // nw):
        y = jnp.dot(h, w_ref[:, c * nw:(c + 1) * nw], preferred_element_type=F32)
        if c * nw < n_qk:
            is_k = c * nw >= RET_HEADS * RET_DK
            for s in range(nw // LANES):
                ys = y[:, s * LANES:(s + 1) * LANES]
                y3 = ys.reshape(rows, GRID_W, LANES)
                r3 = pltpu.roll(ys, LANES // 2, 1).reshape(rows, GRID_W, LANES)
                if s % 2 == 0:
                    o3 = y3 * cr + r3 * sr
                else:
                    o3 = y3 * cc + r3 * scol
                if is_k:
                    o3 = o3 * (RET_DK ** -0.5)
                o_ref[:, c * nw + s * LANES:c * nw + (s + 1) * LANES] = o3.reshape(tm, LANES).astype(BF16)
        else:
            o_ref[:, c * nw:(c + 1) * nw] = y.astype(BF16)


def _inproj0(x, g, sh, sc, w_bf16, seq):
    t, d = x.shape
    n = w_bf16.shape[1]
    tm = TOKEN_TILE
    per_b = seq // tm
    rows = tm // GRID_W
    cos, sin = _rope_tables(seq, LANES)
    n_img_rows = seq // GRID_W
    cr = jnp.asarray(cos[:n_img_rows].reshape(n_img_rows, 1, LANES))
    sr = jnp.asarray(sin[:n_img_rows].reshape(n_img_rows, 1, LANES))
    cc = jnp.asarray(cos[:GRID_W])
    scol = jnp.asarray(sin[:GRID_W])
    vec = lambda: pl.BlockSpec((1, d), lambda i: (0, 0))
    bvec = lambda: pl.BlockSpec((None, 1, d), lambda i: (i // per_b, 0, 0))
    rowt = lambda: pl.BlockSpec((rows, 1, LANES), lambda i: (i % per_b, 0, 0))
    colt = lambda: pl.BlockSpec((GRID_W, LANES), lambda i: (0, 0))
    return pl.pallas_call(
        functools.partial(_inproj0_kernel, tm=tm),
        grid=(t // tm,),
        in_specs=[pl.BlockSpec((tm, d), lambda i: (i, 0)), vec(), bvec(), bvec(),
                  _resident((d, n)), rowt(), rowt(), colt(), colt()],
        out_specs=pl.BlockSpec((tm, n), lambda i: (i, 0)),
        out_shape=jax.ShapeDtypeStruct((t, n), BF16),
        compiler_params=_params(),
        name="l0_inproj",
    )(x, g.reshape(1, d), sh, sc, w_bf16, cr, sr, cc, scol)


def _col(n):
    return lax.broadcasted_iota(jnp.int32, (n, 1), 0).astype(F32)


def _ret_fwd_kernel(lg_ref, q_ref, k_ref, v_ref, y_ref, state_ref, *, nchunks):
    @pl.when(pl.program_id(1) == 0)
    def _():
        state_ref[...] = jnp.zeros_like(state_ref)

    c_len = RET_CHUNK
    ii = lax.broadcasted_iota(jnp.int32, (c_len, c_len), 0)
    jj = lax.broadcasted_iota(jnp.int32, (c_len, c_len), 1)
    diff = (ii - jj).astype(F32)
    idx = _col(c_len)
    consts = []
    for hd in range(RET_HEADS):
        lgf = lg_ref[0, hd]
        lgb = lg_ref[1, hd]
        dmat = jnp.where(diff >= 0, jnp.exp(diff * lgf), jnp.exp(-diff * lgb))
        xi = jnp.exp((idx + 1.0) * lgf)
        zeta = jnp.exp((c_len - 1.0 - idx) * lgf)
        decay = jnp.exp(jnp.full((1, 1), c_len, F32) * lgf)
        consts.append((dmat, xi, zeta, decay))
    for c in range(nchunks):
        sl = slice(c * c_len, (c + 1) * c_len)
        for hd in range(RET_HEADS):
            dmat, xi, zeta, decay = consts[hd]
            qk = slice(hd * RET_DK, (hd + 1) * RET_DK)
            vv = slice(hd * RET_DV, (hd + 1) * RET_DV)
            q = q_ref[sl, qk]
            k = k_ref[sl, qk]
            v = v_ref[sl, vv]
            s = lax.dot_general(q, k, (((1,), (1,)), ((), ())), preferred_element_type=F32) * dmat
            y = jnp.dot(s.astype(BF16), v, preferred_element_type=F32)
            st = state_ref[hd]
            y = y + jnp.dot(q, st.astype(BF16), preferred_element_type=F32) * xi
            kz = (k.astype(F32) * zeta).astype(BF16)
            upd = lax.dot_general(kz, v, (((0,), (0,)), ((), ())), preferred_element_type=F32)
            state_ref[hd] = st * decay + upd
            y_ref[sl, vv] = y.astype(y_ref.dtype)


def _ret_bwd_kernel(lg_ref, q_ref, k_ref, v_ref, g_ref, yf_ref, gn_ref, u_ref, state_ref, *, nchunks):
    @pl.when(pl.program_id(1) == 0)
    def _():
        state_ref[...] = jnp.zeros_like(state_ref)

    c_len = RET_CHUNK
    idx = _col(c_len)
    consts = []
    for hd in range(RET_HEADS):
        lgb = lg_ref[1, hd]
        consts.append((jnp.exp((c_len - idx) * lgb), jnp.exp(idx * lgb),
                       jnp.exp(jnp.full((1, 1), c_len, F32) * lgb)))
    for c in reversed(range(nchunks)):
        sl = slice(c * c_len, (c + 1) * c_len)
        for hd in range(RET_HEADS):
            xi, zeta, decay = consts[hd]
            qk = slice(hd * RET_DK, (hd + 1) * RET_DK)
            vv = slice(hd * RET_DV, (hd + 1) * RET_DV)
            q = q_ref[sl, qk]
            k = k_ref[sl, qk]
            v = v_ref[sl, vv]
            st = state_ref[hd]
            y = yf_ref[sl, vv].astype(F32) + jnp.dot(q, st.astype(BF16), preferred_element_type=F32) * xi
            kz = (k.astype(F32) * zeta).astype(BF16)
            upd = lax.dot_general(kz, v, (((0,), (0,)), ((), ())), preferred_element_type=F32)
            state_ref[hd] = st * decay + upd
            mu = jnp.mean(y, axis=-1, keepdims=True)
            yc = y - mu
            var = jnp.mean(yc * yc, axis=-1, keepdims=True)
            yn = yc * lax.rsqrt(var + NORM_EPS) * gn_ref[:, vv]
            u_ref[sl, vv] = (_silu(g_ref[sl, vv].astype(F32)) * yn).astype(u_ref.dtype)


def _retention(proj, log_gamma, gn_g, bsz, seq):
    t = proj.shape[0]
    tb = RET_BLOCK
    nb = seq // tb
    nchunks = tb // RET_CHUNK
    n_qk = RET_HEADS * RET_DK
    n_v = RET_HEADS * RET_DV
    assert n_v == 2 * n_qk
    grid = (bsz, nb)
    qspec = lambda f: pl.BlockSpec((tb, n_qk), lambda b, n, lg: (b * nb + f(n), 0))
    kspec = lambda f: pl.BlockSpec((tb, n_qk), lambda b, n, lg: (b * nb + f(n), 1))
    vspec = lambda f: pl.BlockSpec((tb, n_v), lambda b, n, lg: (b * nb + f(n), 1))
    gspec = lambda f: pl.BlockSpec((tb, n_v), lambda b, n, lg: (b * nb + f(n), 2))
    yspec = lambda f: pl.BlockSpec((tb, n_v), lambda b, n, lg: (b * nb + f(n), 0))
    fwd = lambda n: n
    rev = lambda n: nb - 1 - n
    state = pltpu.VMEM((RET_HEADS, RET_DK, RET_DV), F32)
    yf = pl.pallas_call(
        functools.partial(_ret_fwd_kernel, nchunks=nchunks),
        grid_spec=pltpu.PrefetchScalarGridSpec(
            num_scalar_prefetch=1, grid=grid,
            in_specs=[qspec(fwd), kspec(fwd), vspec(fwd)],
            out_specs=yspec(fwd), scratch_shapes=[state]),
        out_shape=jax.ShapeDtypeStruct((t, RET_HEADS * RET_DV), BF16),
        compiler_params=_params(),
        name="l0_ret_fwd",
    )(log_gamma, proj, proj, proj)
    return pl.pallas_call(
        functools.partial(_ret_bwd_kernel, nchunks=nchunks),
        grid_spec=pltpu.PrefetchScalarGridSpec(
            num_scalar_prefetch=1, grid=grid,
            in_specs=[qspec(rev), kspec(rev), vspec(rev), gspec(rev), yspec(rev),
                      pl.BlockSpec((1, n_v), lambda b, n, lg: (0, 0))],
            out_specs=yspec(rev), scratch_shapes=[state]),
        out_shape=jax.ShapeDtypeStruct((t, RET_HEADS * RET_DV), BF16),
        compiler_params=_params(),
        name="l0_ret_bwd",
    )(log_gamma, proj, proj, proj, proj, yf, gn_g.reshape(1, -1))


def _top2(logits):
    lane = lax.broadcasted_iota(jnp.int32, logits.shape, 1)
    v1 = jnp.max(logits, axis=-1, keepdims=True)
    i1 = jnp.min(jnp.where(logits == v1, lane, LANES), axis=-1, keepdims=True)
    rest = jnp.where(lane == i1, -jnp.inf, logits)
    v2 = jnp.max(rest, axis=-1, keepdims=True)
    i2 = jnp.min(jnp.where(rest == v2, lane, LANES), axis=-1, keepdims=True)
    e = jnp.exp(v2 - v1)
    w1 = 1.0 / (1.0 + e)
    w2 = e / (1.0 + e)
    idx = jnp.where(lane == 0, i1, jnp.where(lane == 1, i2, 0))
    w = jnp.where(lane == 0, w1, jnp.where(lane == 1, w2, 0.0))
    return idx, w


def _outproj_kernel(u_ref, w_ref, x_ref, gate_ref, g_ref, sh_ref, sc_ref, *rest, router):
    if router:
        wr_ref, br_ref, xo_ref, ho_ref, idx_ref, rw_ref = rest
    else:
        xo_ref, ho_ref = rest
    m = jnp.dot(u_ref[...], w_ref[...], preferred_element_type=F32)
    x1 = x_ref[...] + gate_ref[...] * m
    xo_ref[...] = x1
    h = _rms_mod(x1, g_ref[...], sh_ref[...], sc_ref[...])
    ho_ref[...] = h.astype(ho_ref.dtype)
    if router:
        h_hi = h.astype(BF16)
        h_lo = (h - h_hi.astype(F32)).astype(BF16)
        a = jnp.dot(h_hi, wr_ref[...], preferred_element_type=F32)
        b = jnp.dot(h_lo, wr_ref[:, :LANES], preferred_element_type=F32)
        logits = (a[:, :LANES] + a[:, LANES:]) + b + br_ref[...]
        idx, w = _top2(logits)
        idx_ref[...] = idx
        rw_ref[...] = w


def _outproj(u, w_bf16, x, gate, g, sh, sc, seq, router=None, h_dtype=BF16):
    t, d = x.shape
    kdim = u.shape[1]
    tm = TOKEN_TILE
    per_b = seq // tm
    row = lambda width: pl.BlockSpec((tm, width), lambda i: (i, 0))
    vec = lambda: pl.BlockSpec((1, d), lambda i: (0, 0))
    bvec = lambda: pl.BlockSpec((None, 1, d), lambda i: (i // per_b, 0, 0))
    in_specs = [row(kdim), _resident((kdim, d)), row(d), bvec(), vec(), bvec(), bvec()]
    args = [u, w_bf16, x, gate, g.reshape(1, d), sh, sc]
    out_specs = [row(d), row(d)]
    out_shape = [jax.ShapeDtypeStruct((t, d), F32), jax.ShapeDtypeStruct((t, d), h_dtype)]
    if router is not None:
        wr, br = router
        in_specs += [_resident((d, 2 * LANES)), pl.BlockSpec((1, LANES), lambda i: (0, 0))]
        args += [wr, br]
        out_specs += [row(LANES), row(LANES)]
        out_shape += [jax.ShapeDtypeStruct((t, LANES), jnp.int32), jax.ShapeDtypeStruct((t, LANES), F32)]
    return pl.pallas_call(
        functools.partial(_outproj_kernel, router=router is not None),
        grid=(t // tm,),
        in_specs=in_specs, out_specs=out_specs, out_shape=out_shape,
        compiler_params=_params(),
        name="outproj_router" if router is not None else "outproj",
    )(*args)


def _ffn0_kernel(h_ref, wg_ref, wu_ref, wd_ref, x_ref, gate_ref, g_ref, sh_ref, sc_ref, xo_ref, ho_ref, *, tf):
    h = h_ref[...]
    acc = jnp.zeros(x_ref.shape, F32)
    for c in range(wg_ref.shape[1] // tf):
        cs = slice(c * tf, (c + 1) * tf)
        a = jnp.dot(h, wg_ref[:, cs], preferred_element_type=F32)
        b = jnp.dot(h, wu_ref[:, cs], preferred_element_type=F32)
        act = (_silu(a) * b).astype(BF16)
        acc = acc + jnp.dot(act, wd_ref[cs, :], preferred_element_type=F32)
    x2 = x_ref[...] + gate_ref[...] * acc
    xo_ref[...] = x2
    ho_ref[...] = _rms_mod(x2, g_ref[...], sh_ref[...], sc_ref[...]).astype(ho_ref.dtype)


def _ffn0(h, wg, wu, wd, x, gate, g, sh, sc, seq):
    t, d = x.shape
    f = wg.shape[1]
    tm = TOKEN_TILE
    per_b = seq // tm
    row = lambda: pl.BlockSpec((tm, d), lambda i: (i, 0))
    vec = lambda: pl.BlockSpec((1, d), lambda i: (0, 0))
    bvec = lambda: pl.BlockSpec((None, 1, d), lambda i: (i // per_b, 0, 0))
    return pl.pallas_call(
        functools.partial(_ffn0_kernel, tf=f // 2),
        grid=(t // tm,),
        in_specs=[row(), _resident((d, f)), _resident((d, f)), _resident((f, d)), row(), bvec(), vec(), bvec(), bvec()],
        out_specs=[row(), row()],
        out_shape=[jax.ShapeDtypeStruct((t, d), F32), jax.ShapeDtypeStruct((t, d), BF16)],
        compiler_params=_params(),
        name="l0_ffn",
    )(h, wg, wu, wd, x, gate, g.reshape(1, d), sh, sc)


def _qkv1_kernel(h_ref, w_ref, qg_ref, kg_ref, cr_ref, s1r_ref, s2r_ref, cc_ref, s1c_ref, s2c_ref, o_ref, *, tm):
    rows = tm // GRID_W
    y = jnp.dot(h_ref[...], w_ref[...], preferred_element_type=F32)
    cos =(cr_ref[...] + cc_ref[...][None]).reshape(tm, LANES)
    s1 = (s1r_ref[...] + s1c_ref[...][None]).reshape(tm, LANES)
    s2 = (s2r_ref[...] + s2c_ref[...][None]).reshape(tm, LANES)
    n_rot = ATT_HEADS + ATT_KV_HEADS
    for j in range(n_rot):
        ys = y[:, j * LANES:(j + 1) * LANES]
        gain = qg_ref[...] if j < ATT_HEADS else kg_ref[...]
        yn = ys * lax.rsqrt(jnp.mean(ys * ys, axis=-1, keepdims=True) + NORM_EPS) * gain
        o = yn * cos + pltpu.roll(yn, 96, 1) * s1 + pltpu.roll(yn, 32, 1) * s2
        if j < ATT_HEADS:
            o = o * (ATT_HD ** -0.5 * LOG2_E)
        o_ref[:, j * LANES:(j + 1) * LANES] = o.astype(BF16)
    o_ref[:, n_rot * LANES:] = y[:, n_rot * LANES:].astype(BF16)


def _att_rope_tables(seq):
    half = ATT_HD // 2
    nfreq = half // 2
    freqs = ROPE_THETA ** (-np.arange(0, nfreq, dtype=np.float64) * 2.0 / half)
    lane = np.arange(LANES)
    f_l = freqs[lane % nfreq]
    first = (lane % half) < nfreq
    is_row = lane < half
    n_img_rows = seq // GRID_W

    def tables(npos, mask):
        ang = np.arange(npos, dtype=np.float64)[:, None] * f_l[None, :]
        c = np.where(mask[None], np.cos(ang), 0.0)
        s1 = np.where((mask & first)[None], -np.sin(ang), 0.0)
        s2 = np.where((mask & ~first)[None], np.sin(ang), 0.0)
        return [a.astype(np.float32) for a in (c, s1, s2)]

    row_t = [jnp.asarray(a.reshape(n_img_rows, 1, LANES)) for a in tables(n_img_rows, is_row)]
    col_t = [jnp.asarray(a) for a in tables(GRID_W, ~is_row)]
    return row_t, col_t


def _qkv1(h, w_bf16, qg, kg, seq):
    t, d = h.shape
    n = w_bf16.shape[1]
    tm = TOKEN_TILE
    per_b = seq // tm
    rows = tm // GRID_W
    row_t, col_t = _att_rope_tables(seq)
    rowt = lambda: pl.BlockSpec((rows, 1, LANES), lambda i: (i % per_b, 0, 0))
    colt = lambda: pl.BlockSpec((GRID_W, LANES), lambda i: (0, 0))
    hvec = lambda: pl.BlockSpec((1, LANES), lambda i: (0, 0))
    return pl.pallas_call(
        functools.partial(_qkv1_kernel, tm=tm),
        grid=(t // tm,),
        in_specs=[pl.BlockSpec((tm, d), lambda i: (i, 0)), _resident((d, n)), hvec(), hvec(),
                  rowt(), rowt(), rowt(), colt(), colt(), colt()],
        out_specs=pl.BlockSpec((tm, n), lambda i: (i, 0)),
        out_shape=jax.ShapeDtypeStruct((t, n), BF16),
        compiler_params=_params(),
        name="l1_qkv",
    )(h, w_bf16, qg.reshape(1, LANES), kg.reshape(1, LANES), *row_t, *col_t)


def _flash_kernel(q_ref, k_ref, vt_ref, o_ref, qs_ref, m_ref, l_ref, acc_ref, s_ref, cm_ref, *, tq, tk):
    for g in range(ATT_GROUP):
        qs_ref[g * tq:(g + 1) * tq, :] = q_ref[:, g * LANES:(g + 1) * LANES]
    m_ref[...] = jnp.full(m_ref.shape, -jnp.inf, F32)
    l_ref[...] = jnp.zeros_like(l_ref)
    acc_ref[...] = jnp.zeros_like(acc_ref)
    nk = vt_ref.shape[0]

    def scores(j, slot):
        k_c = k_ref[pl.ds(pl.multiple_of(j * tk, tk), tk), :]
        s = lax.dot_general(k_c, qs_ref[...], (((1,), (1,)), ((), ())), preferred_element_type=F32)
        s_ref[slot] = s
        cm_ref[slot] = jnp.max(s, axis=0, keepdims=True)

    def update(j, slot):
        m_old = m_ref[...]
        m_new = jnp.maximum(m_old, cm_ref[slot])
        p = jnp.exp2(s_ref[slot] - m_new)
        alpha = jnp.exp2(m_old - m_new)
        l_ref[...] = alpha * l_ref[...] + jnp.sum(p, axis=0, keepdims=True)
        acc_ref[...] = alpha * acc_ref[...] + jnp.dot(vt_ref[j], p.astype(BF16), preferred_element_type=F32)
        m_ref[...] = m_new

    scores(0, 0)
    scores(1, 1)

    def body(jj, carry):
        j = ATT_SLOTS * jj
        for u in range(ATT_SLOTS):
            scores(jnp.minimum(j + u + 2, nk - 1), (u + 2) % ATT_SLOTS)
            update(j + u, u)
        return carry

    lax.fori_loop(0, nk // ATT_SLOTS, body, 0)
    o = acc_ref[...] / l_ref[...]
    for g in range(ATT_GROUP):
        o_ref[:, g * LANES:(g + 1) * LANES] = o[:, g * tq:(g + 1) * tq].T.astype(o_ref.dtype)


def _attention(qkv, bsz, seq):
    t = qkv.shape[0]
    tq = ATT_Q_TILE
    tk = min(ATT_K_TILE, seq // ATT_SLOTS)
    nq = seq // tq
    nk = seq // tk
    assert nk % ATT_SLOTS == 0 and nk * tk == seq, "the score pipeline rotates through ATT_SLOTS key chunks"
    gw = ATT_GROUP * ATT_HD
    nqs = ATT_GROUP * tq
    v = qkv[:, (ATT_HEADS + ATT_KV_HEADS) * ATT_HD:]
    vt = v.reshape(bsz, nk, tk, ATT_KV_HEADS, ATT_HD).transpose(0, 3, 1, 4, 2)
    return pl.pallas_call(
        functools.partial(_flash_kernel, tq=tq, tk=tk),
        grid=(bsz, ATT_KV_HEADS, nq),
        in_specs=[
            pl.BlockSpec((tq, gw), lambda b, kv, i: (b * nq + i, kv)),
            pl.BlockSpec((seq, ATT_HD), lambda b, kv, i: (b, ATT_HEADS + kv)),
            pl.BlockSpec((None, None, nk, ATT_HD, tk), lambda b, kv, i: (b, kv, 0, 0, 0)),
        ],
        out_specs=pl.BlockSpec((tq, gw), lambda b, kv, i: (b * nq + i, kv)),
        out_shape=jax.ShapeDtypeStruct((t, ATT_HEADS * ATT_HD), BF16),
        scratch_shapes=[pltpu.VMEM((nqs, ATT_HD), BF16), pltpu.VMEM((1, nqs), F32), pltpu.VMEM((1, nqs), F32),
                        pltpu.VMEM((ATT_HD, nqs), F32), pltpu.VMEM((ATT_SLOTS, tk, nqs), F32),
                        pltpu.VMEM((ATT_SLOTS, 1, nqs), F32)],
        compiler_params=_params(),
        name="l1_attention",
    )(qkv, qkv, vt)


def _dispatch_kernel(lt_ref, dest_ref, h_ref, xs_hbm, zero_ref, zsem, sem, *, tb, tm):
    @pl.when(pl.program_id(0) == 0)
    def _():
        zero_ref[...] = jnp.zeros_like(zero_ref)
        for e in range(2 * N_EXPERTS):
            fill = pltpu.make_async_copy(zero_ref, xs_hbm.at[pl.ds(lt_ref[e] * tm, tm), :], zsem)
            fill.start()
            fill.wait()

    for r in range(tb):
        for k in range(2):
            pltpu.make_async_copy(h_ref.at[pl.ds(r, 1), :], xs_hbm.at[pl.ds(dest_ref[k, r], 1), :], sem).start()
    for k in range(2):
        pltpu.make_async_copy(h_ref, xs_hbm.at[pl.ds(0, tb), :], sem).wait()


def _dispatch(h, dest, last_tile, n_tiles):
    t, d = h.shape
    tb = DISPATCH_TILE
    tm = MOE_TILE
    return pl.pallas_call(
        functools.partial(_dispatch_kernel, tb=tb, tm=tm),
        grid_spec=pltpu.PrefetchScalarGridSpec(
            num_scalar_prefetch=1, grid=(t // tb,),
            in_specs=[
                pl.BlockSpec((None, 2, tb), lambda i, lt: (i, 0, 0), memory_space=pltpu.SMEM),
                pl.BlockSpec((tb, d), lambda i, lt: (i, 0)),
            ],
            out_specs=pl.BlockSpec(memory_space=pl.ANY),
            scratch_shapes=[pltpu.VMEM((tm, d), F32), pltpu.SemaphoreType.DMA(()), pltpu.SemaphoreType.DMA(())]),
        out_shape=jax.ShapeDtypeStruct((n_tiles * tm, d), F32),
        compiler_params=_params(),
        name="l1_moe_dispatch",
    )(last_tile, dest, h)


def _moe_ffn_kernel(te_ref, nu_ref, x_ref, wg_ref, wu_ref, wd_ref, y_ref, xb_ref, acc_ref):
    i = pl.program_id(0)
    j = pl.program_id(1)
    nj = pl.num_programs(1)
    used = i < nu_ref[0]

    @pl.when(jnp.logical_and(used, j == 0))
    def _():
        xb_ref[...] = x_ref[...].astype(BF16)

    @pl.when(used)
    def _():
        xb = xb_ref[...]
        a = jnp.dot(xb, wg_ref[...], preferred_element_type=F32)
        b = jnp.dot(xb, wu_ref[...], preferred_element_type=F32)
        act = (_silu(a) * b).astype(BF16)
        part = jnp.dot(act, wd_ref[...], preferred_element_type=F32)

        @pl.when(j == 0)
        def _():
            acc_ref[...] = part

        @pl.when(jnp.logical_and(j > 0, j < nj - 1))
        def _():
            acc_ref[...] += part

        @pl.when(j == nj - 1)
        def _():
            y_ref[...] = acc_ref[...] + part

    @pl.when(jnp.logical_and(jnp.logical_not(used), j == nj - 1))
    def _():
        y_ref[...] = jnp.zeros_like(y_ref)


def _moe_ffn(xs, tile_expert, n_used, wg, wu, wd):
    d = xs.shape[1]
    tm = MOE_TILE
    n_tiles = xs.shape[0] // tm
    tf = MOE_FF_TILE
    f = wg.shape[2]
    nj = f // tf
    assert nj >= 2 and nj * tf == f

    def jeff(i, j, nu):
        return jnp.where(i < nu[0], j, nj - 1)

    return pl.pallas_call(
        _moe_ffn_kernel,
        grid_spec=pltpu.PrefetchScalarGridSpec(
            num_scalar_prefetch=2, grid=(n_tiles, nj),
            in_specs=[
                pl.BlockSpec((tm, d), lambda i, j, te, nu: (jnp.minimum(i, nu[0] - 1), 0)),
                pl.BlockSpec((None, d, tf), lambda i, j, te, nu: (te[i], 0, jeff(i, j, nu))),
                pl.BlockSpec((None, d, tf), lambda i, j, te, nu: (te[i], 0, jeff(i, j, nu))),
                pl.BlockSpec((None, tf, d), lambda i, j, te, nu: (te[i], jeff(i, j, nu), 0)),
            ],
            out_specs=pl.BlockSpec((tm, d), lambda i, j, te, nu: (i, 0)),
            scratch_shapes=[pltpu.VMEM((tm, d), BF16), pltpu.VMEM((tm, d), F32)]),
        out_shape=jax.ShapeDtypeStruct((n_tiles * tm, d), F32),
        compiler_params=_params(),
        name="l1_moe_ffn",
    )(tile_expert, n_used, xs, wg, wu, wd)


def _combine_kernel(dfirst_ref, dnext_ref, y_hbm, x_ref, rw_ref, gate_ref, g_ref, sh_ref, sc_ref, o_ref,
                    ya_ref, yb_ref, sem, *, tb):
    i = pl.program_id(0)
    n = pl.num_programs(0)
    slot = i % 2

    def row_copies(dest_ref, r, buf_slot):
        ca = pltpu.make_async_copy(y_hbm.at[pl.ds(dest_ref[0, r], 1), :], ya_ref.at[buf_slot, pl.ds(r, 1), :],
                                   sem.at[buf_slot])
        cb = pltpu.make_async_copy(y_hbm.at[pl.ds(dest_ref[1, r], 1), :], yb_ref.at[buf_slot, pl.ds(r, 1), :],
                                   sem.at[buf_slot])
        return ca, cb

    @pl.when(i == 0)
    def _():
        def issue(r, c):
            ca, cb = row_copies(dfirst_ref, r, 0)
            ca.start()
            cb.start()
            return c
        lax.fori_loop(0, tb, issue, 0)

    @pl.when(i + 1 < n)
    def _():
        for r in range(tb):
            ca, cb = row_copies(dnext_ref, r, 1 - slot)
            ca.start()
            cb.start()

    pltpu.make_async_copy(y_hbm.at[pl.ds(0, tb), :], ya_ref.at[slot], sem.at[slot]).wait()
    pltpu.make_async_copy(y_hbm.at[pl.ds(0, tb), :], yb_ref.at[slot], sem.at[slot]).wait()
    rw = rw_ref[...]
    f = rw[:, 0:1] * ya_ref[slot] + rw[:, 1:2] * yb_ref[slot]
    x = x_ref[...] + gate_ref[...] * f
    o_ref[...] = _rms_mod(x, g_ref[...], sh_ref[...], sc_ref[...])


def _combine(dest, ys, x, rw, gate, g, sh, sc, seq):
    t, d = x.shape
    tb = COMBINE_TILE
    per_b = seq // tb
    n = t // tb
    vec = lambda: pl.BlockSpec((1, d), lambda i: (0, 0))
    bvec = lambda: pl.BlockSpec((None, 1, d), lambda i: (i // per_b, 0, 0))
    return pl.pallas_call(
        functools.partial(_combine_kernel, tb=tb),
        grid=(n,),
        in_specs=[
            pl.BlockSpec((None, 2, tb), lambda i: (0, 0, 0), memory_space=pltpu.SMEM),
            pl.BlockSpec((None, 2, tb), lambda i: (jnp.minimum(i + 1, n - 1), 0, 0), memory_space=pltpu.SMEM),
            pl.BlockSpec(memory_space=pl.ANY),
            pl.BlockSpec((tb, d), lambda i: (i, 0)),
            pl.BlockSpec((tb, LANES), lambda i: (i, 0)),
            bvec(), vec(), bvec(), bvec(),
        ],
        out_specs=pl.BlockSpec((tb, d), lambda i: (i, 0)),
        out_shape=jax.ShapeDtypeStruct((t, d), F32),
        scratch_shapes=[pltpu.VMEM((2, tb, d), F32), pltpu.VMEM((2, tb, d), F32), pltpu.SemaphoreType.DMA((2,))],
        compiler_params=_params(),
        name="l1_moe_combine",
    )(dest, dest, ys, x, rw, gate, g.reshape(1, d), sh, sc)


def _moe_plan(idx, t):
    tm = MOE_TILE
    n_tiles = (2 * t) // tm + N_EXPERTS
    e_flat = idx.reshape(-1)
    experts = jnp.arange(N_EXPERTS, dtype=jnp.int32)
    onehot = (e_flat[:, None] == experts[None, :]).astype(jnp.int32)
    incl = jnp.cumsum(onehot, axis=0)
    counts = incl[-1]
    pos = jnp.sum((incl - onehot) * onehot, axis=1)
    tiles_per_e = (counts + tm - 1) // tm
    tile_end = jnp.cumsum(tiles_per_e)
    row_start = (tile_end - tiles_per_e) * tm
    dest = (row_start[e_flat] + pos).astype(jnp.int32)
    n_used = tile_end[-1:]
    tile_ids = jnp.arange(n_tiles, dtype=jnp.int32)
    tile_expert = jnp.minimum(jnp.sum((tile_ids[:, None] >= tile_end[None, :]).astype(jnp.int32), axis=1),
                              N_EXPERTS - 1)
    last_e = jnp.max(jnp.where(counts > 0, experts, 0))
    tile_expert = jnp.where(tile_ids < n_used[0], tile_expert, last_e).astype(jnp.int32)
    last_tile = jnp.maximum(tile_end - 1, 0)
    unused = jnp.minimum(n_used[0] + experts, n_tiles - 1)
    clear_tiles = jnp.concatenate([last_tile, unused]).astype(jnp.int32)
    return dest, tile_expert, n_used.astype(jnp.int32), clear_tiles, n_tiles


def _by_tile(dest, t, tb):
    return dest.reshape(t // tb, tb, 2).transpose(0, 2, 1)


def kernel(x, c, l0_ada_w, l0_ada_b, l0_norm1_g, l0_norm2_g, l0_ret_w_in, l0_ret_decay_logit, l0_ret_gn_g, l0_ret_w_out, l0_ffn_w_gate, l0_ffn_w_up, l0_ffn_w_down, l1_ada_w, l1_ada_b, l1_norm1_g, l1_norm2_g, l1_attn_w_qkv, l1_attn_q_norm_g, l1_attn_k_norm_g, l1_attn_w_out, l1_moe_w_router, l1_moe_b_router, l1_moe_w_gate, l1_moe_w_up, l1_moe_w_down, final_ada_w, final_ada_b, final_norm_g):
    bsz, seq, d = x.shape
    t = bsz * seq
    xf = x.reshape(t, d)

    c8 = jnp.zeros((8, d), F32).at[:bsz].set(c)

    def mods(w, b, n):
        m = _adaln(c8, w, b)[:bsz]
        return [m[:, i * d:(i + 1) * d].reshape(bsz, 1, d) for i in range(n)]

    sh1, sc1, g1, sh2, sc2, g2 = mods(l0_ada_w, l0_ada_b, 6)
    sh3, sc3, g3, sh4, sc4, g4 = mods(l1_ada_w, l1_ada_b, 6)
    fsh, fsc = mods(final_ada_w, final_ada_b, 2)

    bf = lambda w: w.astype(BF16)

    proj = _inproj0(xf, l0_norm1_g, sh1, sc1, bf(l0_ret_w_in), seq)
    log_gamma = jax.nn.log_sigmoid(l0_ret_decay_logit.astype(F32))
    u = _retention(proj, log_gamma, l0_ret_gn_g, bsz, seq)
    x1, h2 = _outproj(u, bf(l0_ret_w_out), xf, g1, l0_norm2_g, sh2, sc2, seq)
    x2, h3 = _ffn0(h2, bf(l0_ffn_w_gate), bf(l0_ffn_w_up), bf(l0_ffn_w_down), x1, g2, l1_norm1_g, sh3, sc3, seq)

    qkv = _qkv1(h3, bf(l1_attn_w_qkv), l1_attn_q_norm_g, l1_attn_k_norm_g, seq)
    o = _attention(qkv, bsz, seq)
    wr = jnp.zeros((d, LANES), F32).at[:, :N_EXPERTS].set(l1_moe_w_router)
    wr_hi = wr.astype(BF16)
    wr_lo = (wr - wr_hi.astype(F32)).astype(BF16)
    br = jnp.full((1, LANES), -jnp.inf, F32).at[0, :N_EXPERTS].set(l1_moe_b_router.astype(F32))
    x3, h4, idx, rw = _outproj(o, bf(l1_attn_w_out), x2, g3, l1_norm2_g, sh4, sc4, seq,
                               router=(jnp.concatenate([wr_hi, wr_lo], axis=1), br), h_dtype=F32)
    dest, tile_expert, n_used, clear_tiles, n_tiles = _moe_plan(idx[:, :2], t)
    xs = _dispatch(h4, _by_tile(dest, t, DISPATCH_TILE), clear_tiles, n_tiles)
    ys = _moe_ffn(xs, tile_expert, n_used, bf(l1_moe_w_gate), bf(l1_moe_w_up), bf(l1_moe_w_down))
    out = _combine(_by_tile(dest, t, COMBINE_TILE), ys, x3, rw, g4, final_norm_g, fsh, fsc, seq)
    return out.reshape(bsz, seq, d)
```

```python
import functools

import numpy as np
import jax
import jax.numpy as jnp
from jax import lax
from jax.experimental import pallas as pl
from jax.experimental.pallas import tpu as pltpu

GRID_W = 64
NORM_EPS = 1e-6
ROPE_THETA = 10000.0
RET_HEADS = 4
RET_DK = 256
RET_DV = 512
RET_CHUNK = 128
ATT_HEADS = 8
ATT_KV_HEADS = 2
ATT_HD = 128
ATT_GROUP = ATT_HEADS // ATT_KV_HEADS
N_EXPERTS = 8
LANES = 128
LOG2_E = 1.4426950408889634

F32 = jnp.float32
BF16 = jnp.bfloat16
HIGHEST = lax.Precision.HIGHEST

TOKEN_TILE = 512
RET_BLOCK = 512
ATT_Q_TILE = 256
ATT_K_TILE = 1024
ATT_SLOTS = 4
ATT_ONES_ROWS = 16
ATT_BOUND_LIMIT = 60.0
MOE_TILE = 512
MOE_FF_TILE = 1792
DISPATCH_TILE = 512
COMBINE_TILE = 256
VMEM_LIMIT = 56 * 1024 * 1024


def _params(**kw):
    return pltpu.CompilerParams(vmem_limit_bytes=VMEM_LIMIT, **kw)


def _resident(shape):
    return pl.BlockSpec(shape, lambda *_: (0,) * len(shape), pipeline_mode=pl.Buffered(1))


def _silu(a):
    return a * jax.nn.sigmoid(a)


def _rms_mod(x, g, shift, scale):
    y = x * lax.rsqrt(jnp.mean(x * x, axis=-1, keepdims=True) + NORM_EPS)
    return (y * g) * (1.0 + scale) + shift


def _adaln_kernel(c_ref, w_ref, b_ref, o_ref):
    a = _silu(c_ref[...])
    o_ref[...] = jnp.dot(a, w_ref[...], precision=HIGHEST, preferred_element_type=F32) + b_ref[...]


def _adaln(c8, w, b):
    d, n = w.shape
    tn = 1024
    return pl.pallas_call(
        _adaln_kernel,
        grid=(n // tn,),
        in_specs=[
            pl.BlockSpec((8, d), lambda j: (0, 0)),
            pl.BlockSpec((d, tn), lambda j: (0, j)),
            pl.BlockSpec((1, tn), lambda j: (0, j)),
        ],
        out_specs=pl.BlockSpec((8, tn), lambda j: (0, j)),
        out_shape=jax.ShapeDtypeStruct((8, n), F32),
        compiler_params=_params(),
        name="adaln",
    )(c8, w, b.reshape(1, n))


def _rope_tables(seq, half):
    nfreq = half // 2
    freqs = ROPE_THETA ** (-np.arange(0, nfreq, dtype=np.float64) * 2.0 / half)
    npos = max(seq // GRID_W, GRID_W)
    ang = np.arange(npos, dtype=np.float64)[:, None] * freqs[None, :]
    cos = np.concatenate([np.cos(ang), np.cos(ang)], axis=1)
    sin = np.concatenate([-np.sin(ang), np.sin(ang)], axis=1)
    return cos.astype(np.float32), sin.astype(np.float32)


def _inproj0_kernel(x_ref, g_ref, sh_ref, sc_ref, w_ref, cr_ref, sr_ref, cc_ref, scol_ref, o_ref, *, tm):
    h = _rms_mod(x_ref[...], g_ref[...], sh_ref[...], sc_ref[...]).astype(BF16)
    rows = tm // GRID_W
    cr = cr_ref[...]
    sr = sr_ref[...]
    cc = cc_ref[...][None]
    scol = scol_ref[...][None]
    nw = 512
    n_qk = 2 * RET_HEADS * RET_DK
    for c in range(o_ref.shape[1] // nw):
        y = jnp.dot(h, w_ref[:, c * nw:(c + 1) * nw], preferred_element_type=F32)
        if c * nw < n_qk:
            is_k = c * nw >= RET_HEADS * RET_DK
            for s in range(nw // LANES):
                ys = y[:, s * LANES:(s + 1) * LANES]
                y3 = ys.reshape(rows, GRID_W, LANES)
                r3 = pltpu.roll(ys, LANES // 2, 1).reshape(rows, GRID_W, LANES)
                if s % 2 == 0:
                    o3 = y3 * cr + r3 * sr
                else:
                    o3 = y3 * cc + r3 * scol
                if is_k:
                    o3 = o3 * (RET_DK ** -0.5)
                o_ref[:, c * nw + s * LANES:c * nw + (s + 1) * LANES] = o3.reshape(tm, LANES).astype(BF16)
        else:
            o_ref[:, c * nw:(c + 1) * nw] = y.astype(BF16)


def _inproj0(x, g, sh, sc, w_bf16, seq):
    t, d = x.shape
    n = w_bf16.shape[1]
    tm = TOKEN_TILE
    per_b = seq // tm
    rows = tm // GRID_W
    cos, sin = _rope_tables(seq, LANES)
    n_img_rows = seq // GRID_W
    cr = jnp.asarray(cos[:n_img_rows].reshape(n_img_rows, 1, LANES))
    sr = jnp.asarray(sin[:n_img_rows].reshape(n_img_rows, 1, LANES))
    cc = jnp.asarray(cos[:GRID_W])
    scol = jnp.asarray(sin[:GRID_W])
    vec = lambda: pl.BlockSpec((1, d), lambda i: (0, 0))
    bvec = lambda: pl.BlockSpec((None, 1, d), lambda i: (i // per_b, 0, 0))
    rowt = lambda: pl.BlockSpec((rows, 1, LANES), lambda i: (i % per_b, 0, 0))
    colt = lambda: pl.BlockSpec((GRID_W, LANES), lambda i: (0, 0))
    return pl.pallas_call(
        functools.partial(_inproj0_kernel, tm=tm),
        grid=(t // tm,),
        in_specs=[pl.BlockSpec((tm, d), lambda i: (i, 0)), vec(), bvec(), bvec(),
                  _resident((d, n)), rowt(), rowt(), colt(), colt()],
        out_specs=pl.BlockSpec((tm, n), lambda i: (i, 0)),
        out_shape=jax.ShapeDtypeStruct((t, n), BF16),
        compiler_params=_params(),
        name="l0_inproj",
    )(x, g.reshape(1, d), sh, sc, w_bf16, cr, sr, cc, scol)


def _col(n):
    return lax.broadcasted_iota(jnp.int32, (n, 1), 0).astype(F32)


def _ret_fwd_kernel(lg_ref, q_ref, k_ref, v_ref, y_ref, state_ref, *, nchunks):
    @pl.when(pl.program_id(1) == 0)
    def _():
        state_ref[...] = jnp.zeros_like(state_ref)

    c_len = RET_CHUNK
    ii = lax.broadcasted_iota(jnp.int32, (c_len, c_len), 0)
    jj = lax.broadcasted_iota(jnp.int32, (c_len, c_len), 1)
    diff = (ii - jj).astype(F32)
    idx = _col(c_len)
    consts = []
    for hd in range(RET_HEADS):
        lgf = lg_ref[0, hd]
        lgb = lg_ref[1, hd]
        dmat = jnp.where(diff >= 0, jnp.exp(diff * lgf), jnp.exp(-diff * lgb))
        xi = jnp.exp((idx + 1.0) * lgf)
        zeta = jnp.exp((c_len - 1.0 - idx) * lgf)
        decay = jnp.exp(jnp.full((1, 1), c_len, F32) * lgf)
        consts.append((dmat, xi, zeta, decay))
    for c in range(nchunks):
        sl = slice(c * c_len, (c + 1) * c_len)
        for hd in range(RET_HEADS):
            dmat, xi, zeta, decay = consts[hd]
            qk = slice(hd * RET_DK, (hd + 1) * RET_DK)
            vv = slice(hd * RET_DV, (hd + 1) * RET_DV)
            q = q_ref[sl, qk]
            k = k_ref[sl, qk]
            v = v_ref[sl, vv]
            s = lax.dot_general(q, k, (((1,), (1,)), ((), ())), preferred_element_type=F32) * dmat
            y = jnp.dot(s.astype(BF16), v, preferred_element_type=F32)
            st = state_ref[hd]
            y = y + jnp.dot(q, st.astype(BF16), preferred_element_type=F32) * xi
            kz = (k.astype(F32) * zeta).astype(BF16)
            upd = lax.dot_general(kz, v, (((0,), (0,)), ((), ())), preferred_element_type=F32)
            state_ref[hd] = st * decay + upd
            y_ref[sl, vv] = y.astype(y_ref.dtype)


def _ret_bwd_kernel(lg_ref, q_ref, k_ref, v_ref, g_ref, yf_ref, gn_ref, u_ref, state_ref, *, nchunks):
    @pl.when(pl.program_id(1) == 0)
    def _():
        state_ref[...] = jnp.zeros_like(state_ref)

    c_len = RET_CHUNK
    idx = _col(c_len)
    consts = []
    for hd in range(RET_HEADS):
        lgb = lg_ref[1, hd]
        consts.append((jnp.exp((c_len - idx) * lgb), jnp.exp(idx * lgb),
                       jnp.exp(jnp.full((1, 1), c_len, F32) * lgb)))
    for c in reversed(range(nchunks)):
        sl = slice(c * c_len, (c + 1) * c_len)
        for hd in range(RET_HEADS):
            xi, zeta, decay = consts[hd]
            qk = slice(hd * RET_DK, (hd + 1) * RET_DK)
            vv = slice(hd * RET_DV, (hd + 1) * RET_DV)
            q = q_ref[sl, qk]
            k = k_ref[sl, qk]
            v = v_ref[sl, vv]
            st = state_ref[hd]
            y = yf_ref[sl, vv].astype(F32) + jnp.dot(q, st.astype(BF16), preferred_element_type=F32) * xi
            kz = (k.astype(F32) * zeta).astype(BF16)
            upd = lax.dot_general(kz, v, (((0,), (0,)), ((), ())), preferred_element_type=F32)
            state_ref[hd] = st * decay + upd
            mu = jnp.mean(y, axis=-1, keepdims=True)
            yc = y - mu
            var = jnp.mean(yc * yc, axis=-1, keepdims=True)
            yn = yc * lax.rsqrt(var + NORM_EPS) * gn_ref[:, vv]
            u_ref[sl, vv] = (_silu(g_ref[sl, vv].astype(F32)) * yn).astype(u_ref.dtype)


def _retention(proj, log_gamma, gn_g, bsz, seq):
    t = proj.shape[0]
    tb = RET_BLOCK
    nb = seq // tb
    nchunks = tb // RET_CHUNK
    n_qk = RET_HEADS * RET_DK
    n_v = RET_HEADS * RET_DV
    assert n_v == 2 * n_qk
    grid = (bsz, nb)
    qspec = lambda f: pl.BlockSpec((tb, n_qk), lambda b, n, lg: (b * nb + f(n), 0))
    kspec = lambda f: pl.BlockSpec((tb, n_qk), lambda b, n, lg: (b * nb + f(n), 1))
    vspec = lambda f: pl.BlockSpec((tb, n_v), lambda b, n, lg: (b * nb + f(n), 1))
    gspec = lambda f: pl.BlockSpec((tb, n_v), lambda b, n, lg: (b * nb + f(n), 2))
    yspec = lambda f: pl.BlockSpec((tb, n_v), lambda b, n, lg: (b * nb + f(n), 0))
    fwd = lambda n: n
    rev = lambda n: nb - 1 - n
    state = pltpu.VMEM((RET_HEADS, RET_DK, RET_DV), F32)
    yf = pl.pallas_call(
        functools.partial(_ret_fwd_kernel, nchunks=nchunks),
        grid_spec=pltpu.PrefetchScalarGridSpec(
            num_scalar_prefetch=1, grid=grid,
            in_specs=[qspec(fwd), kspec(fwd), vspec(fwd)],
            out_specs=yspec(fwd), scratch_shapes=[state]),
        out_shape=jax.ShapeDtypeStruct((t, RET_HEADS * RET_DV), BF16),
        compiler_params=_params(),
        name="l0_ret_fwd",
    )(log_gamma, proj, proj, proj)
    return pl.pallas_call(
        functools.partial(_ret_bwd_kernel, nchunks=nchunks),
        grid_spec=pltpu.PrefetchScalarGridSpec(
            num_scalar_prefetch=1, grid=grid,
            in_specs=[qspec(rev), kspec(rev), vspec(rev), gspec(rev), yspec(rev),
                      pl.BlockSpec((1, n_v), lambda b, n, lg: (0, 0))],
            out_specs=yspec(rev), scratch_shapes=[state]),
        out_shape=jax.ShapeDtypeStruct((t, RET_HEADS * RET_DV), BF16),
        compiler_params=_params(),
        name="l0_ret_bwd",
    )(log_gamma, proj, proj, proj, proj, yf, gn_g.reshape(1, -1))


def _top2(logits):
    lane = lax.broadcasted_iota(jnp.int32, logits.shape, 1)
    v1 = jnp.max(logits, axis=-1, keepdims=True)
    i1 = jnp.min(jnp.where(logits == v1, lane, LANES), axis=-1, keepdims=True)
    rest = jnp.where(lane == i1, -jnp.inf, logits)
    v2 = jnp.max(rest, axis=-1, keepdims=True)
    i2 = jnp.min(jnp.where(rest == v2, lane, LANES), axis=-1, keepdims=True)
    e = jnp.exp(v2 - v1)
    w1 = 1.0 / (1.0 + e)
    w2 = e / (1.0 + e)
    idx = jnp.where(lane == 0, i1, jnp.where(lane == 1, i2, 0))
    w = jnp.where(lane == 0, w1, jnp.where(lane == 1, w2, 0.0))
    return idx, w


def _outproj_kernel(u_ref, w_ref, x_ref, gate_ref, g_ref, sh_ref, sc_ref, *rest, router):
    if router:
        wr_ref, br_ref, xo_ref, ho_ref, idx_ref, rw_ref = rest
    else:
        xo_ref, ho_ref = rest
    m = jnp.dot(u_ref[...], w_ref[...], preferred_element_type=F32)
    x1 = x_ref[...] + gate_ref[...] * m
    xo_ref[...] = x1
    h = _rms_mod(x1, g_ref[...], sh_ref[...], sc_ref[...])
    ho_ref[...] = h.astype(ho_ref.dtype)
    if router:
        h_hi = h.astype(BF16)
        h_lo = (h - h_hi.astype(F32)).astype(BF16)
        a = jnp.dot(h_hi, wr_ref[...], preferred_element_type=F32)
        b = jnp.dot(h_lo, wr_ref[:, :LANES], preferred_element_type=F32)
        logits = (a[:, :LANES] + a[:, LANES:]) + b + br_ref[...]
        idx, w = _top2(logits)
        idx_ref[...] = idx
        rw_ref[...] = w


def _outproj(u, w_bf16, x, gate, g, sh, sc, seq, router=None, h_dtype=BF16):
    t, d = x.shape
    kdim = u.shape[1]
    tm = TOKEN_TILE
    per_b = seq // tm
    row = lambda width: pl.BlockSpec((tm, width), lambda i: (i, 0))
    vec = lambda: pl.BlockSpec((1, d), lambda i: (0, 0))
    bvec = lambda: pl.BlockSpec((None, 1, d), lambda i: (i // per_b, 0, 0))
    in_specs = [row(kdim), _resident((kdim, d)), row(d), bvec(), vec(), bvec(), bvec()]
    args = [u, w_bf16, x, gate, g.reshape(1, d), sh, sc]
    out_specs = [row(d), row(d)]
    out_shape = [jax.ShapeDtypeStruct((t, d), F32), jax.ShapeDtypeStruct((t, d), h_dtype)]
    if router is not None:
        wr, br = router
        in_specs += [_resident((d, 2 * LANES)), pl.BlockSpec((1, LANES), lambda i: (0, 0))]
        args += [wr, br]
        out_specs += [row(LANES), row(LANES)]
        out_shape += [jax.ShapeDtypeStruct((t, LANES), jnp.int32), jax.ShapeDtypeStruct((t, LANES), F32)]
    return pl.pallas_call(
        functools.partial(_outproj_kernel, router=router is not None),
        grid=(t // tm,),
        in_specs=in_specs, out_specs=out_specs, out_shape=out_shape,
        compiler_params=_params(),
        name="outproj_router" if router is not None else "outproj",
    )(*args)


def _ffn0_kernel(h_ref, wg_ref, wu_ref, wd_ref, x_ref, gate_ref, g_ref, sh_ref, sc_ref, xo_ref, ho_ref, *, tf):
    h = h_ref[...]
    acc = jnp.zeros(x_ref.shape, F32)
    for c in range(wg_ref.shape[1] // tf):
        cs = slice(c * tf, (c + 1) * tf)
        a = jnp.dot(h, wg_ref[:, cs], preferred_element_type=F32)
        b = jnp.dot(h, wu_ref[:, cs], preferred_element_type=F32)
        act = (_silu(a) * b).astype(BF16)
        acc = acc + jnp.dot(act, wd_ref[cs, :], preferred_element_type=F32)
    x2 = x_ref[...] + gate_ref[...] * acc
    xo_ref[...] = x2
    ho_ref[...] = _rms_mod(x2, g_ref[...], sh_ref[...], sc_ref[...]).astype(ho_ref.dtype)


def _ffn0(h, wg, wu, wd, x, gate, g, sh, sc, seq):
    t, d = x.shape
    f = wg.shape[1]
    tm = TOKEN_TILE
    per_b = seq // tm
    row = lambda: pl.BlockSpec((tm, d), lambda i: (i, 0))
    vec = lambda: pl.BlockSpec((1, d), lambda i: (0, 0))
    bvec = lambda: pl.BlockSpec((None, 1, d), lambda i: (i // per_b, 0, 0))
    return pl.pallas_call(
        functools.partial(_ffn0_kernel, tf=f // 2),
        grid=(t // tm,),
        in_specs=[row(), _resident((d, f)), _resident((d, f)), _resident((f, d)), row(), bvec(), vec(), bvec(), bvec()],
        out_specs=[row(), row()],
        out_shape=[jax.ShapeDtypeStruct((t, d), F32), jax.ShapeDtypeStruct((t, d), BF16)],
        compiler_params=_params(),
        name="l0_ffn",
    )(h, wg, wu, wd, x, gate, g.reshape(1, d), sh, sc)


def _qkv1_kernel(h_ref, w_ref, qg_ref, kg_ref, cr_ref, s1r_ref, s2r_ref, cc_ref, s1c_ref, s2c_ref, o_ref, n2_ref,
                 *, tm):
    rows = tm // GRID_W
    y = jnp.dot(h_ref[...], w_ref[...], preferred_element_type=F32)
    cos =(cr_ref[...] + cc_ref[...][None]).reshape(tm, LANES)
    s1 = (s1r_ref[...] + s1c_ref[...][None]).reshape(tm, LANES)
    s2 = (s2r_ref[...] + s2c_ref[...][None]).reshape(tm, LANES)
    n_rot = ATT_HEADS + ATT_KV_HEADS
    lane = lax.broadcasted_iota(jnp.int32, (tm, LANES), 1)
    norm2 = jnp.zeros((tm, LANES), F32)
    for j in range(n_rot):
        ys = y[:, j * LANES:(j + 1) * LANES]
        gain = qg_ref[...] if j < ATT_HEADS else kg_ref[...]
        yn = ys * lax.rsqrt(jnp.mean(ys * ys, axis=-1, keepdims=True) + NORM_EPS) * gain
        o = yn * cos + pltpu.roll(yn, 96, 1) * s1 + pltpu.roll(yn, 32, 1) * s2
        if j < ATT_HEADS:
            o = o * (ATT_HD ** -0.5 * LOG2_E)
        ob = o.astype(BF16)
        o_ref[:, j * LANES:(j + 1) * LANES] = ob
        obf = ob.astype(F32)
        norm2 = jnp.where(lane == j, jnp.sum(obf * obf, axis=-1, keepdims=True), norm2)
    o_ref[:, n_rot * LANES:] = y[:, n_rot * LANES:].astype(BF16)
    n2_ref[...] = norm2


def _att_rope_tables(seq):
    half = ATT_HD // 2
    nfreq = half // 2
    freqs = ROPE_THETA ** (-np.arange(0, nfreq, dtype=np.float64) * 2.0 / half)
    lane = np.arange(LANES)
    f_l = freqs[lane % nfreq]
    first = (lane % half) < nfreq
    is_row = lane < half
    n_img_rows = seq // GRID_W

    def tables(npos, mask):
        ang = np.arange(npos, dtype=np.float64)[:, None] * f_l[None, :]
        c = np.where(mask[None], np.cos(ang), 0.0)
        s1 = np.where((mask & first)[None], -np.sin(ang), 0.0)
        s2 = np.where((mask & ~first)[None], np.sin(ang), 0.0)
        return [a.astype(np.float32) for a in (c, s1, s2)]

    row_t = [jnp.asarray(a.reshape(n_img_rows, 1, LANES)) for a in tables(n_img_rows, is_row)]
    col_t = [jnp.asarray(a) for a in tables(GRID_W, ~is_row)]
    return row_t, col_t


def _qkv1(h, w_bf16, qg, kg, seq):
    t, d = h.shape
    n = w_bf16.shape[1]
    tm = TOKEN_TILE
    per_b = seq // tm
    rows = tm // GRID_W
    row_t, col_t = _att_rope_tables(seq)
    rowt = lambda: pl.BlockSpec((rows, 1, LANES), lambda i: (i % per_b, 0, 0))
    colt = lambda: pl.BlockSpec((GRID_W, LANES), lambda i: (0, 0))
    hvec = lambda: pl.BlockSpec((1, LANES), lambda i: (0, 0))
    return pl.pallas_call(
        functools.partial(_qkv1_kernel, tm=tm),
        grid=(t // tm,),
        in_specs=[pl.BlockSpec((tm, d), lambda i: (i, 0)), _resident((d, n)), hvec(), hvec(),
                  rowt(), rowt(), rowt(), colt(), colt(), colt()],
        out_specs=[pl.BlockSpec((tm, n), lambda i: (i, 0)), pl.BlockSpec((tm, LANES), lambda i: (i, 0))],
        out_shape=[jax.ShapeDtypeStruct((t, n), BF16), jax.ShapeDtypeStruct((t, LANES), F32)],
        compiler_params=_params(),
        name="l1_qkv",
    )(h, w_bf16, qg.reshape(1, LANES), kg.reshape(1, LANES), *row_t, *col_t)


def _stack_heads(q_ref, qs_ref, tq):
    for g in range(ATT_GROUP):
        qs_ref[g * tq:(g + 1) * tq, :] = q_ref[:, g * LANES:(g + 1) * LANES]


def _pipelined(nk, scores, update):
    scores(0, 0)
    scores(1, 1)

    def body(jj, carry):
        j = ATT_SLOTS * jj
        for u in range(ATT_SLOTS):
            scores(jnp.minimum(j + u + 2, nk - 1), (u + 2) % ATT_SLOTS)
            update(j + u, u)
        return carry

    lax.fori_loop(0, nk // ATT_SLOTS, body, 0)


def _store_heads(o, o_ref, tq):
    for g in range(ATT_GROUP):
        o_ref[:, g * LANES:(g + 1) * LANES] = o[:, g * tq:(g + 1) * tq].T.astype(o_ref.dtype)


def _flash_bounded_kernel(q_ref, k_ref, vt_ref, mb_ref, o_ref, qs_ref, acc_ref, p_ref, vte_ref, *, tq, tk):
    nk = vt_ref.shape[0]

    @pl.when(pl.program_id(2) == 0)
    def _():
        for j in range(nk):
            vte_ref[j, :ATT_HD, :] = vt_ref[j]
            vte_ref[j, ATT_HD:, :] = jnp.ones((ATT_ONES_ROWS, tk), BF16)

    _stack_heads(q_ref, qs_ref, tq)
    acc_ref[...] = jnp.zeros_like(acc_ref)
    mb = mb_ref[...]

    def scores(j, slot):
        k_c = k_ref[pl.ds(pl.multiple_of(j * tk, tk), tk), :]
        s = lax.dot_general(k_c, qs_ref[...], (((1,), (1,)), ((), ())), preferred_element_type=F32)
        p_ref[slot] = jnp.exp2(s - mb).astype(BF16)

    def update(j, slot):
        acc_ref[...] += jnp.dot(vte_ref[j], p_ref[slot], preferred_element_type=F32)

    _pipelined(nk, scores, update)
    _store_heads(acc_ref[:ATT_HD, :] / acc_ref[ATT_HD:ATT_HD + 1, :], o_ref, tq)


def _flash_online_kernel(q_ref, k_ref, vt_ref, mb_ref, o_ref, qs_ref, m_ref, l_ref, acc_ref, s_ref, cm_ref, *, tq, tk):
    del mb_ref
    _stack_heads(q_ref, qs_ref, tq)
    m_ref[...] = jnp.full(m_ref.shape, -jnp.inf, F32)
    l_ref[...] = jnp.zeros_like(l_ref)
    acc_ref[...] = jnp.zeros_like(acc_ref)
    nk = vt_ref.shape[0]

    def scores(j, slot):
        k_c = k_ref[pl.ds(pl.multiple_of(j * tk, tk), tk), :]
        s = lax.dot_general(k_c, qs_ref[...], (((1,), (1,)), ((), ())), preferred_element_type=F32)
        s_ref[slot] = s
        cm_ref[slot] = jnp.max(s, axis=0, keepdims=True)

    def update(j, slot):
        m_old = m_ref[...]
        m_new = jnp.maximum(m_old, cm_ref[slot])
        p = jnp.exp2(s_ref[slot] - m_new)
        alpha = jnp.exp2(m_old - m_new)
        l_ref[...] = alpha * l_ref[...] + jnp.sum(p, axis=0, keepdims=True)
        acc_ref[...] = alpha * acc_ref[...] + jnp.dot(vt_ref[j], p.astype(BF16), preferred_element_type=F32)
        m_ref[...] = m_new

    _pipelined(nk, scores, update)
    _store_heads(acc_ref[...] / l_ref[...], o_ref, tq)


def _attention(qkv, norm2, bsz, seq):
    t = qkv.shape[0]
    tq = ATT_Q_TILE
    tk = min(ATT_K_TILE, seq // ATT_SLOTS)
    nq = seq // tq
    nk = seq // tk
    assert nk % ATT_SLOTS == 0 and nk * tk == seq, "the score pipeline rotates through ATT_SLOTS key chunks"
    gw = ATT_GROUP * ATT_HD
    nqs = ATT_GROUP * tq
    v = qkv[:, (ATT_HEADS + ATT_KV_HEADS) * ATT_HD:]
    vt = v.reshape(bsz, nk, tk, ATT_KV_HEADS, ATT_HD).transpose(0, 3, 1, 4, 2)

    qn = jnp.sqrt(norm2[:, :ATT_HEADS]).reshape(bsz, nq, tq, ATT_KV_HEADS, ATT_GROUP)
    kn = jnp.sqrt(jnp.max(norm2[:, ATT_HEADS:ATT_HEADS + ATT_KV_HEADS].reshape(bsz, seq, ATT_KV_HEADS), axis=1))
    mb = qn.transpose(0, 3, 1, 4, 2) * kn[:, :, None, None, None]
    mb = mb.reshape(bsz * ATT_KV_HEADS * nq, 1, nqs)

    def call(body, scratch):
        return pl.pallas_call(
            functools.partial(body, tq=tq, tk=tk),
            grid=(bsz, ATT_KV_HEADS, nq),
            in_specs=[
                pl.BlockSpec((tq, gw), lambda b, kv, i: (b * nq + i, kv)),
                pl.BlockSpec((seq, ATT_HD), lambda b, kv, i: (b, ATT_HEADS + kv)),
                pl.BlockSpec((None, None, nk, ATT_HD, tk), lambda b, kv, i: (b, kv, 0, 0, 0)),
                pl.BlockSpec((None, 1, nqs), lambda b, kv, i: ((b * ATT_KV_HEADS + kv) * nq + i, 0, 0)),
            ],
            out_specs=pl.BlockSpec((tq, gw), lambda b, kv, i: (b * nq + i, kv)),
            out_shape=jax.ShapeDtypeStruct((t, ATT_HEADS * ATT_HD), BF16),
            scratch_shapes=[pltpu.VMEM((nqs, ATT_HD), BF16)] + scratch,
            compiler_params=_params(),
            name="l1_attention",
        )(qkv, qkv, vt, mb)

    def bounded():
        return call(_flash_bounded_kernel, [pltpu.VMEM((ATT_HD + ATT_ONES_ROWS, nqs), F32),
                                            pltpu.VMEM((ATT_SLOTS, tk, nqs), BF16),
                                            pltpu.VMEM((nk, ATT_HD + ATT_ONES_ROWS, tk), BF16)])

    def online():
        return call(_flash_online_kernel, [pltpu.VMEM((1, nqs), F32), pltpu.VMEM((1, nqs), F32),
                                           pltpu.VMEM((ATT_HD, nqs), F32), pltpu.VMEM((ATT_SLOTS, tk, nqs), F32),
                                           pltpu.VMEM((ATT_SLOTS, 1, nqs), F32)])

    return lax.cond(jnp.max(mb) <= ATT_BOUND_LIMIT, bounded, online)


def _dispatch_kernel(lt_ref, dest_ref, h_ref, xs_hbm, zero_ref, zsem, sem, *, tb, tm):
    @pl.when(pl.program_id(0) == 0)
    def _():
        zero_ref[...] = jnp.zeros_like(zero_ref)
        for e in range(2 * N_EXPERTS):
            fill = pltpu.make_async_copy(zero_ref, xs_hbm.at[pl.ds(lt_ref[e] * tm, tm), :], zsem)
            fill.start()
            fill.wait()

    for r in range(tb):
        for k in range(2):
            pltpu.make_async_copy(h_ref.at[pl.ds(r, 1), :], xs_hbm.at[pl.ds(dest_ref[k, r], 1), :], sem).start()
    for k in range(2):
        pltpu.make_async_copy(h_ref, xs_hbm.at[pl.ds(0, tb), :], sem).wait()


def _dispatch(h, dest, last_tile, n_tiles):
    t, d = h.shape
    tb = DISPATCH_TILE
    tm = MOE_TILE
    return pl.pallas_call(
        functools.partial(_dispatch_kernel, tb=tb, tm=tm),
        grid_spec=pltpu.PrefetchScalarGridSpec(
            num_scalar_prefetch=1, grid=(t // tb,),
            in_specs=[
                pl.BlockSpec((None, 2, tb), lambda i, lt: (i, 0, 0), memory_space=pltpu.SMEM),
                pl.BlockSpec((tb, d), lambda i, lt: (i, 0)),
            ],
            out_specs=pl.BlockSpec(memory_space=pl.ANY),
            scratch_shapes=[pltpu.VMEM((tm, d), F32), pltpu.SemaphoreType.DMA(()), pltpu.SemaphoreType.DMA(())]),
        out_shape=jax.ShapeDtypeStruct((n_tiles * tm, d), F32),
        compiler_params=_params(),
        name="l1_moe_dispatch",
    )(last_tile, dest, h)


def _moe_ffn_kernel(te_ref, nu_ref, x_ref, wg_ref, wu_ref, wd_ref, y_ref, xb_ref, acc_ref):
    i = pl.program_id(0)
    j = pl.program_id(1)
    nj = pl.num_programs(1)
    used = i < nu_ref[0]

    @pl.when(jnp.logical_and(used, j == 0))
    def _():
        xb_ref[...] = x_ref[...].astype(BF16)

    @pl.when(used)
    def _():
        xb = xb_ref[...]
        a = jnp.dot(xb, wg_ref[...], preferred_element_type=F32)
        b = jnp.dot(xb, wu_ref[...], preferred_element_type=F32)
        act = (_silu(a) * b).astype(BF16)
        part = jnp.dot(act, wd_ref[...], preferred_element_type=F32)

        @pl.when(j == 0)
        def _():
            acc_ref[...] = part

        @pl.when(jnp.logical_and(j > 0, j < nj - 1))
        def _():
            acc_ref[...] += part

        @pl.when(j == nj - 1)
        def _():
            y_ref[...] = acc_ref[...] + part

    @pl.when(jnp.logical_and(jnp.logical_not(used), j == nj - 1))
    def _():
        y_ref[...] = jnp.zeros_like(y_ref)


def _moe_ffn(xs, tile_expert, n_used, wg, wu, wd):
    d = xs.shape[1]
    tm = MOE_TILE
    n_tiles = xs.shape[0] // tm
    tf = MOE_FF_TILE
    f = wg.shape[2]
    nj = f // tf
    assert nj >= 2 and nj * tf == f

    def jeff(i, j, nu):
        return jnp.where(i < nu[0], j, nj - 1)

    return pl.pallas_call(
        _moe_ffn_kernel,
        grid_spec=pltpu.PrefetchScalarGridSpec(
            num_scalar_prefetch=2, grid=(n_tiles, nj),
            in_specs=[
                pl.BlockSpec((tm, d), lambda i, j, te, nu: (jnp.minimum(i, nu[0] - 1), 0)),
                pl.BlockSpec((None, d, tf), lambda i, j, te, nu: (te[i], 0, jeff(i, j, nu))),
                pl.BlockSpec((None, d, tf), lambda i, j, te, nu: (te[i], 0, jeff(i, j, nu))),
                pl.BlockSpec((None, tf, d), lambda i, j, te, nu: (te[i], jeff(i, j, nu), 0)),
            ],
            out_specs=pl.BlockSpec((tm, d), lambda i, j, te, nu: (i, 0)),
            scratch_shapes=[pltpu.VMEM((tm, d), BF16), pltpu.VMEM((tm, d), F32)]),
        out_shape=jax.ShapeDtypeStruct((n_tiles * tm, d), F32),
        compiler_params=_params(),
        name="l1_moe_ffn",
    )(tile_expert, n_used, xs, wg, wu, wd)


def _combine_kernel(dfirst_ref, dnext_ref, y_hbm, x_ref, rw_ref, gate_ref, g_ref, sh_ref, sc_ref, o_ref,
                    ya_ref, yb_ref, sem, *, tb):
    i = pl.program_id(0)
    n = pl.num_programs(0)
    slot = i % 2

    def row_copies(dest_ref, r, buf_slot):
        ca = pltpu.make_async_copy(y_hbm.at[pl.ds(dest_ref[0, r], 1), :], ya_ref.at[buf_slot, pl.ds(r, 1), :],
                                   sem.at[buf_slot])
        cb = pltpu.make_async_copy(y_hbm.at[pl.ds(dest_ref[1, r], 1), :], yb_ref.at[buf_slot, pl.ds(r, 1), :],
                                   sem.at[buf_slot])
        return ca, cb

    @pl.when(i == 0)
    def _():
        def issue(r, c):
            ca, cb = row_copies(dfirst_ref, r, 0)
            ca.start()
            cb.start()
            return c
        lax.fori_loop(0, tb, issue, 0)

    @pl.when(i + 1 < n)
    def _():
        for r in range(tb):
            ca, cb = row_copies(dnext_ref, r, 1 - slot)
            ca.start()
            cb.start()

    pltpu.make_async_copy(y_hbm.at[pl.ds(0, tb), :], ya_ref.at[slot], sem.at[slot]).wait()
    pltpu.make_async_copy(y_hbm.at[pl.ds(0, tb), :], yb_ref.at[slot], sem.at[slot]).wait()
    rw = rw_ref[...]
    f = rw[:, 0:1] * ya_ref[slot] + rw[:, 1:2] * yb_ref[slot]
    x = x_ref[...] + gate_ref[...] * f
    o_ref[...] = _rms_mod(x, g_ref[...], sh_ref[...], sc_ref[...])


def _combine(dest, ys, x, rw, gate, g, sh, sc, seq):
    t, d = x.shape
    tb = COMBINE_TILE
    per_b = seq // tb
    n = t // tb
    vec = lambda: pl.BlockSpec((1, d), lambda i: (0, 0))
    bvec = lambda: pl.BlockSpec((None, 1, d), lambda i: (i // per_b, 0, 0))
    return pl.pallas_call(
        functools.partial(_combine_kernel, tb=tb),
        grid=(n,),
        in_specs=[
            pl.BlockSpec((None, 2, tb), lambda i: (0, 0, 0), memory_space=pltpu.SMEM),
            pl.BlockSpec((None, 2, tb), lambda i: (jnp.minimum(i + 1, n - 1), 0, 0), memory_space=pltpu.SMEM),
            pl.BlockSpec(memory_space=pl.ANY),
            pl.BlockSpec((tb, d), lambda i: (i, 0)),
            pl.BlockSpec((tb, LANES), lambda i: (i, 0)),
            bvec(), vec(), bvec(), bvec(),
        ],
        out_specs=pl.BlockSpec((tb, d), lambda i: (i, 0)),
        out_shape=jax.ShapeDtypeStruct((t, d), F32),
        scratch_shapes=[pltpu.VMEM((2, tb, d), F32), pltpu.VMEM((2, tb, d), F32), pltpu.SemaphoreType.DMA((2,))],
        compiler_params=_params(),
        name="l1_moe_combine",
    )(dest, dest, ys, x, rw, gate, g.reshape(1, d), sh, sc)


def _moe_plan(idx, t):
    tm = MOE_TILE
    n_tiles = (2 * t) // tm + N_EXPERTS
    e_flat = idx.reshape(-1)
    experts = jnp.arange(N_EXPERTS, dtype=jnp.int32)
    onehot = (e_flat[:, None] == experts[None, :]).astype(jnp.int32)
    incl = jnp.cumsum(onehot, axis=0)
    counts = incl[-1]
    pos = jnp.sum((incl - onehot) * onehot, axis=1)
    tiles_per_e = (counts + tm - 1) // tm
    tile_end = jnp.cumsum(tiles_per_e)
    row_start = (tile_end - tiles_per_e) * tm
    dest = (row_start[e_flat] + pos).astype(jnp.int32)
    n_used = tile_end[-1:]
    tile_ids = jnp.arange(n_tiles, dtype=jnp.int32)
    tile_expert = jnp.minimum(jnp.sum((tile_ids[:, None] >= tile_end[None, :]).astype(jnp.int32), axis=1),
                              N_EXPERTS - 1)
    last_e = jnp.max(jnp.where(counts > 0, experts, 0))
    tile_expert = jnp.where(tile_ids < n_used[0], tile_expert, last_e).astype(jnp.int32)
    last_tile = jnp.maximum(tile_end - 1, 0)
    unused = jnp.minimum(n_used[0] + experts, n_tiles - 1)
    clear_tiles = jnp.concatenate([last_tile, unused]).astype(jnp.int32)
    return dest, tile_expert, n_used.astype(jnp.int32), clear_tiles, n_tiles


def _by_tile(dest, t, tb):
    return dest.reshape(t // tb, tb, 2).transpose(0, 2, 1)


def kernel(x, c, l0_ada_w, l0_ada_b, l0_norm1_g, l0_norm2_g, l0_ret_w_in, l0_ret_decay_logit, l0_ret_gn_g, l0_ret_w_out, l0_ffn_w_gate, l0_ffn_w_up, l0_ffn_w_down, l1_ada_w, l1_ada_b, l1_norm1_g, l1_norm2_g, l1_attn_w_qkv, l1_attn_q_norm_g, l1_attn_k_norm_g, l1_attn_w_out, l1_moe_w_router, l1_moe_b_router, l1_moe_w_gate, l1_moe_w_up, l1_moe_w_down, final_ada_w, final_ada_b, final_norm_g):
    bsz, seq, d = x.shape
    t = bsz * seq
    xf = x.reshape(t, d)

    c8 = jnp.zeros((8, d), F32).at[:bsz].set(c)

    def mods(w, b, n):
        m = _adaln(c8, w, b)[:bsz]
        return [m[:, i * d:(i + 1) * d].reshape(bsz, 1, d) for i in range(n)]

    sh1, sc1, g1, sh2, sc2, g2 = mods(l0_ada_w, l0_ada_b, 6)
    sh3, sc3, g3, sh4, sc4, g4 = mods(l1_ada_w, l1_ada_b, 6)
    fsh, fsc = mods(final_ada_w, final_ada_b, 2)

    bf = lambda w: w.astype(BF16)

    proj = _inproj0(xf, l0_norm1_g, sh1, sc1, bf(l0_ret_w_in), seq)
    log_gamma = jax.nn.log_sigmoid(l0_ret_decay_logit.astype(F32))
    u = _retention(proj, log_gamma, l0_ret_gn_g, bsz, seq)
    x1, h2 = _outproj(u, bf(l0_ret_w_out), xf, g1, l0_norm2_g, sh2, sc2, seq)
    x2, h3 = _ffn0(h2, bf(l0_ffn_w_gate), bf(l0_ffn_w_up), bf(l0_ffn_w_down), x1, g2, l1_norm1_g, sh3, sc3, seq)

    qkv, norm2 = _qkv1(h3, bf(l1_attn_w_qkv), l1_attn_q_norm_g, l1_attn_k_norm_g, seq)
    o = _attention(qkv, norm2, bsz, seq)
    wr = jnp.zeros((d, LANES), F32).at[:, :N_EXPERTS].set(l1_moe_w_router)
    wr_hi = wr.astype(BF16)
    wr_lo = (wr - wr_hi.astype(F32)).astype(BF16)
    br = jnp.full((1, LANES), -jnp.inf, F32).at[0, :N_EXPERTS].set(l1_moe_b_router.astype(F32))
    x3, h4, idx, rw = _outproj(o, bf(l1_attn_w_out), x2, g3, l1_norm2_g, sh4, sc4, seq,
                               router=(jnp.concatenate([wr_hi, wr_lo], axis=1), br), h_dtype=F32)
    dest, tile_expert, n_used, clear_tiles, n_tiles = _moe_plan(idx[:, :2], t)
    xs = _dispatch(h4, _by_tile(dest, t, DISPATCH_TILE), clear_tiles, n_tiles)
    ys = _moe_ffn(xs, tile_expert, n_used, bf(l1_moe_w_gate), bf(l1_moe_w_up), bf(l1_moe_w_down))
    out = _combine(_by_tile(dest, t, COMBINE_TILE), ys, x3, rw, g4, final_norm_g, fsh, fsc, seq)
    return out.reshape(bsz, seq, d)
```

```python
import functools

import numpy as np
import jax
import jax.numpy as jnp
from jax import lax
from jax.experimental import pallas as pl
from jax.experimental.pallas import tpu as pltpu

GRID_W = 64
NORM_EPS = 1e-6
ROPE_THETA = 10000.0
RET_HEADS = 4
RET_DK = 256
RET_DV = 512
RET_CHUNK = 128
ATT_HEADS = 8
ATT_KV_HEADS = 2
ATT_HD = 128
ATT_GROUP = ATT_HEADS // ATT_KV_HEADS
N_EXPERTS = 8
LANES = 128
LOG2_E = 1.4426950408889634

F32 = jnp.float32
BF16 = jnp.bfloat16
HIGHEST = lax.Precision.HIGHEST

TOKEN_TILE = 512
RET_BLOCK = 512
ATT_Q_TILE_BOUNDED = 512
ATT_Q_TILE_ONLINE = 256
ATT_K_TILE = 1024
ATT_SLOTS = 4
ATT_ONES_ROWS = 16
ATT_BOUND_LIMIT = 60.0
MOE_TILE = 512
MOE_FF_TILE = 1792
DISPATCH_TILE = 512
COMBINE_TILE = 256
VMEM_LIMIT = 56 * 1024 * 1024


def _params(**kw):
    return pltpu.CompilerParams(vmem_limit_bytes=VMEM_LIMIT, **kw)


def _resident(shape):
    return pl.BlockSpec(shape, lambda *_: (0,) * len(shape), pipeline_mode=pl.Buffered(1))


def _silu(a):
    return a * jax.nn.sigmoid(a)


def _rms_mod(x, g, shift, scale):
    y = x * lax.rsqrt(jnp.mean(x * x, axis=-1, keepdims=True) + NORM_EPS)
    return (y * g) * (1.0 + scale) + shift


def _adaln_kernel(c_ref, w_ref, b_ref, o_ref):
    a = _silu(c_ref[...])
    o_ref[...] = jnp.dot(a, w_ref[...], precision=HIGHEST, preferred_element_type=F32) + b_ref[...]


def _adaln(c8, w, b):
    d, n = w.shape
    tn = 1024
    return pl.pallas_call(
        _adaln_kernel,
        grid=(n // tn,),
        in_specs=[
            pl.BlockSpec((8, d), lambda j: (0, 0)),
            pl.BlockSpec((d, tn), lambda j: (0, j)),
            pl.BlockSpec((1, tn), lambda j: (0, j)),
        ],
        out_specs=pl.BlockSpec((8, tn), lambda j: (0, j)),
        out_shape=jax.ShapeDtypeStruct((8, n), F32),
        compiler_params=_params(),
        name="adaln",
    )(c8, w, b.reshape(1, n))


def _rope_tables(seq, half):
    nfreq = half // 2
    freqs = ROPE_THETA ** (-np.arange(0, nfreq, dtype=np.float64) * 2.0 / half)
    npos = max(seq // GRID_W, GRID_W)
    ang = np.arange(npos, dtype=np.float64)[:, None] * freqs[None, :]
    cos = np.concatenate([np.cos(ang), np.cos(ang)], axis=1)
    sin = np.concatenate([-np.sin(ang), np.sin(ang)], axis=1)
    return cos.astype(np.float32), sin.astype(np.float32)


def _inproj0_kernel(x_ref, g_ref, sh_ref, sc_ref, w_ref, cr_ref, sr_ref, cc_ref, scol_ref, o_ref, *, tm):
    h = _rms_mod(x_ref[...], g_ref[...], sh_ref[...], sc_ref[...]).astype(BF16)
    rows = tm // GRID_W
    cr = cr_ref[...]
    sr = sr_ref[...]
    cc = cc_ref[...][None]
    scol = scol_ref[...][None]
    nw = 512
    n_qk = 2 * RET_HEADS * RET_DK
    for c in range(o_ref.shape[1] // nw):
        y = jnp.dot(h, w_ref[:, c * nw:(c + 1) * nw], preferred_element_type=F32)
        if c * nw < n_qk:
            is_k = c * nw >= RET_HEADS * RET_DK
            for s in range(nw // LANES):
                ys = y[:, s * LANES:(s + 1) * LANES]
                y3 = ys.reshape(rows, GRID_W, LANES)
                r3 = pltpu.roll(ys, LANES // 2, 1).reshape(rows, GRID_W, LANES)
                if s % 2 == 0:
                    o3 = y3 * cr + r3 * sr
                else:
                    o3 = y3 * cc + r3 * scol
                if is_k:
                    o3 = o3 * (RET_DK ** -0.5)
                o_ref[:, c * nw + s * LANES:c * nw + (s + 1) * LANES] = o3.reshape(tm, LANES).astype(BF16)
        else:
            o_ref[:, c * nw:(c + 1) * nw] = y.astype(BF16)


def _inproj0(x, g, sh, sc, w_bf16, seq):
    t, d = x.shape
    n = w_bf16.shape[1]
    tm = TOKEN_TILE
    per_b = seq // tm
    rows = tm // GRID_W
    cos, sin = _rope_tables(seq, LANES)
    n_img_rows = seq // GRID_W
    cr = jnp.asarray(cos[:n_img_rows].reshape(n_img_rows, 1, LANES))
    sr = jnp.asarray(sin[:n_img_rows].reshape(n_img_rows, 1, LANES))
    cc = jnp.asarray(cos[:GRID_W])
    scol = jnp.asarray(sin[:GRID_W])
    vec = lambda: pl.BlockSpec((1, d), lambda i: (0, 0))
    bvec = lambda: pl.BlockSpec((None, 1, d), lambda i: (i // per_b, 0, 0))
    rowt = lambda: pl.BlockSpec((rows, 1, LANES), lambda i: (i % per_b, 0, 0))
    colt = lambda: pl.BlockSpec((GRID_W, LANES), lambda i: (0, 0))
    return pl.pallas_call(
        functools.partial(_inproj0_kernel, tm=tm),
        grid=(t // tm,),
        in_specs=[pl.BlockSpec((tm, d), lambda i: (i, 0)), vec(), bvec(), bvec(),
                  _resident((d, n)), rowt(), rowt(), colt(), colt()],
        out_specs=pl.BlockSpec((tm, n), lambda i: (i, 0)),
        out_shape=jax.ShapeDtypeStruct((t, n), BF16),
        compiler_params=_params(),
        name="l0_inproj",
    )(x, g.reshape(1, d), sh, sc, w_bf16, cr, sr, cc, scol)


def _col(n):
    return lax.broadcasted_iota(jnp.int32, (n, 1), 0).astype(F32)


def _ret_fwd_kernel(lg_ref, q_ref, k_ref, v_ref, y_ref, state_ref, *, nchunks):
    @pl.when(pl.program_id(1) == 0)
    def _():
        state_ref[...] = jnp.zeros_like(state_ref)

    c_len = RET_CHUNK
    ii = lax.broadcasted_iota(jnp.int32, (c_len, c_len), 0)
    jj = lax.broadcasted_iota(jnp.int32, (c_len, c_len), 1)
    diff = (ii - jj).astype(F32)
    idx = _col(c_len)
    consts = []
    for hd in range(RET_HEADS):
        lgf = lg_ref[0, hd]
        lgb = lg_ref[1, hd]
        dmat = jnp.where(diff >= 0, jnp.exp(diff * lgf), jnp.exp(-diff * lgb))
        xi = jnp.exp((idx + 1.0) * lgf)
        zeta = jnp.exp((c_len - 1.0 - idx) * lgf)
        decay = jnp.exp(jnp.full((1, 1), c_len, F32) * lgf)
        consts.append((dmat, xi, zeta, decay))
    for c in range(nchunks):
        sl = slice(c * c_len, (c + 1) * c_len)
        for hd in range(RET_HEADS):
            dmat, xi, zeta, decay = consts[hd]
            qk = slice(hd * RET_DK, (hd + 1) * RET_DK)
            vv = slice(hd * RET_DV, (hd + 1) * RET_DV)
            q = q_ref[sl, qk]
            k = k_ref[sl, qk]
            v = v_ref[sl, vv]
            s = lax.dot_general(q, k, (((1,), (1,)), ((), ())), preferred_element_type=F32) * dmat
            y = jnp.dot(s.astype(BF16), v, preferred_element_type=F32)
            st = state_ref[hd]
            y = y + jnp.dot(q, st.astype(BF16), preferred_element_type=F32) * xi
            kz = (k.astype(F32) * zeta).astype(BF16)
            upd = lax.dot_general(kz, v, (((0,), (0,)), ((), ())), preferred_element_type=F32)
            state_ref[hd] = st * decay + upd
            y_ref[sl, vv] = y.astype(y_ref.dtype)


def _ret_bwd_kernel(lg_ref, q_ref, k_ref, v_ref, g_ref, yf_ref, gn_ref, u_ref, state_ref, *, nchunks):
    @pl.when(pl.program_id(1) == 0)
    def _():
        state_ref[...] = jnp.zeros_like(state_ref)

    c_len = RET_CHUNK
    idx = _col(c_len)
    consts = []
    for hd in range(RET_HEADS):
        lgb = lg_ref[1, hd]
        consts.append((jnp.exp((c_len - idx) * lgb), jnp.exp(idx * lgb),
                       jnp.exp(jnp.full((1, 1), c_len, F32) * lgb)))
    for c in reversed(range(nchunks)):
        sl = slice(c * c_len, (c + 1) * c_len)
        for hd in range(RET_HEADS):
            xi, zeta, decay = consts[hd]
            qk = slice(hd * RET_DK, (hd + 1) * RET_DK)
            vv = slice(hd * RET_DV, (hd + 1) * RET_DV)
            q = q_ref[sl, qk]
            k = k_ref[sl, qk]
            v = v_ref[sl, vv]
            st = state_ref[hd]
            y = yf_ref[sl, vv].astype(F32) + jnp.dot(q, st.astype(BF16), preferred_element_type=F32) * xi
            kz = (k.astype(F32) * zeta).astype(BF16)
            upd = lax.dot_general(kz, v, (((0,), (0,)), ((), ())), preferred_element_type=F32)
            state_ref[hd] = st * decay + upd
            mu = jnp.mean(y, axis=-1, keepdims=True)
            yc = y - mu
            var = jnp.mean(yc * yc, axis=-1, keepdims=True)
            yn = yc * lax.rsqrt(var + NORM_EPS) * gn_ref[:, vv]
            u_ref[sl, vv] = (_silu(g_ref[sl, vv].astype(F32)) * yn).astype(u_ref.dtype)


def _retention(proj, log_gamma, gn_g, bsz, seq):
    t = proj.shape[0]
    tb = RET_BLOCK
    nb = seq // tb
    nchunks = tb // RET_CHUNK
    n_qk = RET_HEADS * RET_DK
    n_v = RET_HEADS * RET_DV
    assert n_v == 2 * n_qk
    grid = (bsz, nb)
    qspec = lambda f: pl.BlockSpec((tb, n_qk), lambda b, n, lg: (b * nb + f(n), 0))
    kspec = lambda f: pl.BlockSpec((tb, n_qk), lambda b, n, lg: (b * nb + f(n), 1))
    vspec = lambda f: pl.BlockSpec((tb, n_v), lambda b, n, lg: (b * nb + f(n), 1))
    gspec = lambda f: pl.BlockSpec((tb, n_v), lambda b, n, lg: (b * nb + f(n), 2))
    yspec = lambda f: pl.BlockSpec((tb, n_v), lambda b, n, lg: (b * nb + f(n), 0))
    fwd = lambda n: n
    rev = lambda n: nb - 1 - n
    state = pltpu.VMEM((RET_HEADS, RET_DK, RET_DV), F32)
    yf = pl.pallas_call(
        functools.partial(_ret_fwd_kernel, nchunks=nchunks),
        grid_spec=pltpu.PrefetchScalarGridSpec(
            num_scalar_prefetch=1, grid=grid,
            in_specs=[qspec(fwd), kspec(fwd), vspec(fwd)],
            out_specs=yspec(fwd), scratch_shapes=[state]),
        out_shape=jax.ShapeDtypeStruct((t, RET_HEADS * RET_DV), BF16),
        compiler_params=_params(),
        name="l0_ret_fwd",
    )(log_gamma, proj, proj, proj)
    return pl.pallas_call(
        functools.partial(_ret_bwd_kernel, nchunks=nchunks),
        grid_spec=pltpu.PrefetchScalarGridSpec(
            num_scalar_prefetch=1, grid=grid,
            in_specs=[qspec(rev), kspec(rev), vspec(rev), gspec(rev), yspec(rev),
                      pl.BlockSpec((1, n_v), lambda b, n, lg: (0, 0))],
            out_specs=yspec(rev), scratch_shapes=[state]),
        out_shape=jax.ShapeDtypeStruct((t, RET_HEADS * RET_DV), BF16),
        compiler_params=_params(),
        name="l0_ret_bwd",
    )(log_gamma, proj, proj, proj, proj, yf, gn_g.reshape(1, -1))


def _top2(logits):
    lane = lax.broadcasted_iota(jnp.int32, logits.shape, 1)
    v1 = jnp.max(logits, axis=-1, keepdims=True)
    i1 = jnp.min(jnp.where(logits == v1, lane, LANES), axis=-1, keepdims=True)
    rest = jnp.where(lane == i1, -jnp.inf, logits)
    v2 = jnp.max(rest, axis=-1, keepdims=True)
    i2 = jnp.min(jnp.where(rest == v2, lane, LANES), axis=-1, keepdims=True)
    e = jnp.exp(v2 - v1)
    w1 = 1.0 / (1.0 + e)
    w2 = e / (1.0 + e)
    idx = jnp.where(lane == 0, i1, jnp.where(lane == 1, i2, 0))
    w = jnp.where(lane == 0, w1, jnp.where(lane == 1, w2, 0.0))
    return idx, w


def _outproj_kernel(u_ref, w_ref, x_ref, gate_ref, g_ref, sh_ref, sc_ref, *rest, router):
    if router:
        wr_ref, br_ref, xo_ref, ho_ref, idx_ref, rw_ref = rest
    else:
        xo_ref, ho_ref = rest
    m = jnp.dot(u_ref[...], w_ref[...], preferred_element_type=F32)
    x1 = x_ref[...] + gate_ref[...] * m
    xo_ref[...] = x1
    h = _rms_mod(x1, g_ref[...], sh_ref[...], sc_ref[...])
    ho_ref[...] = h.astype(ho_ref.dtype)
    if router:
        h_hi = h.astype(BF16)
        h_lo = (h - h_hi.astype(F32)).astype(BF16)
        a = jnp.dot(h_hi, wr_ref[...], preferred_element_type=F32)
        b = jnp.dot(h_lo, wr_ref[:, :LANES], preferred_element_type=F32)
        logits = (a[:, :LANES] + a[:, LANES:]) + b + br_ref[...]
        idx, w = _top2(logits)
        idx_ref[...] = idx
        rw_ref[...] = w


def _outproj(u, w_bf16, x, gate, g, sh, sc, seq, router=None, h_dtype=BF16):
    t, d = x.shape
    kdim = u.shape[1]
    tm = TOKEN_TILE
    per_b = seq // tm
    row = lambda width: pl.BlockSpec((tm, width), lambda i: (i, 0))
    vec = lambda: pl.BlockSpec((1, d), lambda i: (0, 0))
    bvec = lambda: pl.BlockSpec((None, 1, d), lambda i: (i // per_b, 0, 0))
    in_specs = [row(kdim), _resident((kdim, d)), row(d), bvec(), vec(), bvec(), bvec()]
    args = [u, w_bf16, x, gate, g.reshape(1, d), sh, sc]
    out_specs = [row(d), row(d)]
    out_shape = [jax.ShapeDtypeStruct((t, d), F32), jax.ShapeDtypeStruct((t, d), h_dtype)]
    if router is not None:
        wr, br = router
        in_specs += [_resident((d, 2 * LANES)), pl.BlockSpec((1, LANES), lambda i: (0, 0))]
        args += [wr, br]
        out_specs += [row(LANES), row(LANES)]
        out_shape += [jax.ShapeDtypeStruct((t, LANES), jnp.int32), jax.ShapeDtypeStruct((t, LANES), F32)]
    return pl.pallas_call(
        functools.partial(_outproj_kernel, router=router is not None),
        grid=(t // tm,),
        in_specs=in_specs, out_specs=out_specs, out_shape=out_shape,
        compiler_params=_params(),
        name="outproj_router" if router is not None else "outproj",
    )(*args)


def _ffn0_kernel(h_ref, wg_ref, wu_ref, wd_ref, x_ref, gate_ref, g_ref, sh_ref, sc_ref, xo_ref, ho_ref, *, tf):
    h = h_ref[...]
    acc = jnp.zeros(x_ref.shape, F32)
    for c in range(wg_ref.shape[1] // tf):
        cs = slice(c * tf, (c + 1) * tf)
        a = jnp.dot(h, wg_ref[:, cs], preferred_element_type=F32)
        b = jnp.dot(h, wu_ref[:, cs], preferred_element_type=F32)
        act = (_silu(a) * b).astype(BF16)
        acc = acc + jnp.dot(act, wd_ref[cs, :], preferred_element_type=F32)
    x2 = x_ref[...] + gate_ref[...] * acc
    xo_ref[...] = x2
    ho_ref[...] = _rms_mod(x2, g_ref[...], sh_ref[...], sc_ref[...]).astype(ho_ref.dtype)


def _ffn0(h, wg, wu, wd, x, gate, g, sh, sc, seq):
    t, d = x.shape
    f = wg.shape[1]
    tm = TOKEN_TILE
    per_b = seq // tm
    row = lambda: pl.BlockSpec((tm, d), lambda i: (i, 0))
    vec = lambda: pl.BlockSpec((1, d), lambda i: (0, 0))
    bvec = lambda: pl.BlockSpec((None, 1, d), lambda i: (i // per_b, 0, 0))
    return pl.pallas_call(
        functools.partial(_ffn0_kernel, tf=f // 2),
        grid=(t // tm,),
        in_specs=[row(), _resident((d, f)), _resident((d, f)), _resident((f, d)), row(), bvec(), vec(), bvec(), bvec()],
        out_specs=[row(), row()],
        out_shape=[jax.ShapeDtypeStruct((t, d), F32), jax.ShapeDtypeStruct((t, d), BF16)],
        compiler_params=_params(),
        name="l0_ffn",
    )(h, wg, wu, wd, x, gate, g.reshape(1, d), sh, sc)


def _qkv1_kernel(h_ref, w_ref, qg_ref, kg_ref, cr_ref, s1r_ref, s2r_ref, cc_ref, s1c_ref, s2c_ref, o_ref, n2_ref,
                 *, tm):
    rows = tm // GRID_W
    y = jnp.dot(h_ref[...], w_ref[...], preferred_element_type=F32)
    cos =(cr_ref[...] + cc_ref[...][None]).reshape(tm, LANES)
    s1 = (s1r_ref[...] + s1c_ref[...][None]).reshape(tm, LANES)
    s2 = (s2r_ref[...] + s2c_ref[...][None]).reshape(tm, LANES)
    n_rot = ATT_HEADS + ATT_KV_HEADS
    lane = lax.broadcasted_iota(jnp.int32, (tm, LANES), 1)
    norm2 = jnp.zeros((tm, LANES), F32)
    for j in range(n_rot):
        ys = y[:, j * LANES:(j + 1) * LANES]
        gain = qg_ref[...] if j < ATT_HEADS else kg_ref[...]
        yn = ys * lax.rsqrt(jnp.mean(ys * ys, axis=-1, keepdims=True) + NORM_EPS) * gain
        o = yn * cos + pltpu.roll(yn, 96, 1) * s1 + pltpu.roll(yn, 32, 1) * s2
        if j < ATT_HEADS:
            o = o * (ATT_HD ** -0.5 * LOG2_E)
        ob = o.astype(BF16)
        o_ref[:, j * LANES:(j + 1) * LANES] = ob
        obf = ob.astype(F32)
        norm2 = jnp.where(lane == j, jnp.sum(obf * obf, axis=-1, keepdims=True), norm2)
    o_ref[:, n_rot * LANES:] = y[:, n_rot * LANES:].astype(BF16)
    n2_ref[...] = norm2


def _att_rope_tables(seq):
    half = ATT_HD // 2
    nfreq = half // 2
    freqs = ROPE_THETA ** (-np.arange(0, nfreq, dtype=np.float64) * 2.0 / half)
    lane = np.arange(LANES)
    f_l = freqs[lane % nfreq]
    first = (lane % half) < nfreq
    is_row = lane < half
    n_img_rows = seq // GRID_W

    def tables(npos, mask):
        ang = np.arange(npos, dtype=np.float64)[:, None] * f_l[None, :]
        c = np.where(mask[None], np.cos(ang), 0.0)
        s1 = np.where((mask & first)[None], -np.sin(ang), 0.0)
        s2 = np.where((mask & ~first)[None], np.sin(ang), 0.0)
        return [a.astype(np.float32) for a in (c, s1, s2)]

    row_t = [jnp.asarray(a.reshape(n_img_rows, 1, LANES)) for a in tables(n_img_rows, is_row)]
    col_t = [jnp.asarray(a) for a in tables(GRID_W, ~is_row)]
    return row_t, col_t


def _qkv1(h, w_bf16, qg, kg, seq):
    t, d = h.shape
    n = w_bf16.shape[1]
    tm = TOKEN_TILE
    per_b = seq // tm
    rows = tm // GRID_W
    row_t, col_t = _att_rope_tables(seq)
    rowt = lambda: pl.BlockSpec((rows, 1, LANES), lambda i: (i % per_b, 0, 0))
    colt = lambda: pl.BlockSpec((GRID_W, LANES), lambda i: (0, 0))
    hvec = lambda: pl.BlockSpec((1, LANES), lambda i: (0, 0))
    return pl.pallas_call(
        functools.partial(_qkv1_kernel, tm=tm),
        grid=(t // tm,),
        in_specs=[pl.BlockSpec((tm, d), lambda i: (i, 0)), _resident((d, n)), hvec(), hvec(),
                  rowt(), rowt(), rowt(), colt(), colt(), colt()],
        out_specs=[pl.BlockSpec((tm, n), lambda i: (i, 0)), pl.BlockSpec((tm, LANES), lambda i: (i, 0))],
        out_shape=[jax.ShapeDtypeStruct((t, n), BF16), jax.ShapeDtypeStruct((t, LANES), F32)],
        compiler_params=_params(),
        name="l1_qkv",
    )(h, w_bf16, qg.reshape(1, LANES), kg.reshape(1, LANES), *row_t, *col_t)


def _stack_heads(q_ref, qs_ref, tq):
    for g in range(ATT_GROUP):
        qs_ref[g * tq:(g + 1) * tq, :] = q_ref[:, g * LANES:(g + 1) * LANES]


def _pipelined(nk, scores, update):
    scores(0, 0)
    scores(1, 1)

    def body(jj, carry):
        j = ATT_SLOTS * jj
        for u in range(ATT_SLOTS):
            scores(j + u + 2, (u + 2) % ATT_SLOTS)
            update(j + u, u)
        return carry

    lax.fori_loop(0, nk // ATT_SLOTS - 1, body, 0)
    j = nk - ATT_SLOTS
    for u in range(ATT_SLOTS):
        if u + 2 < ATT_SLOTS:
            scores(j + u + 2, (u + 2) % ATT_SLOTS)
        update(j + u, u)


def _store_heads(o, o_ref, tq):
    for g in range(ATT_GROUP):
        o_ref[:, g * LANES:(g + 1) * LANES] = o[:, g * tq:(g + 1) * tq].T.astype(o_ref.dtype)


def _flash_bounded_kernel(q_ref, k_ref, vt_ref, mb_ref, o_ref, qs_ref, acc_ref, p_ref, vte_ref, *, tq, tk):
    nk = vt_ref.shape[0]

    @pl.when(pl.program_id(2) == 0)
    def _():
        for j in range(nk):
            vte_ref[j, :ATT_HD, :] = vt_ref[j]
            vte_ref[j, ATT_HD:, :] = jnp.ones((ATT_ONES_ROWS, tk), BF16)

    _stack_heads(q_ref, qs_ref, tq)
    acc_ref[...] = jnp.zeros_like(acc_ref)
    mb = mb_ref[...]

    def scores(j, slot):
        k_c = k_ref[pl.ds(pl.multiple_of(j * tk, tk), tk), :]
        s = lax.dot_general(k_c, qs_ref[...], (((1,), (1,)), ((), ())), preferred_element_type=F32)
        p_ref[slot] = jnp.exp2(s - mb).astype(BF16)

    def update(j, slot):
        acc_ref[...] += jnp.dot(vte_ref[j], p_ref[slot], preferred_element_type=F32)

    _pipelined(nk, scores, update)
    _store_heads(acc_ref[:ATT_HD, :] / acc_ref[ATT_HD:ATT_HD + 1, :], o_ref, tq)


def _flash_online_kernel(q_ref, k_ref, vt_ref, o_ref, qs_ref, m_ref, l_ref, acc_ref, s_ref, cm_ref, *, tq, tk):
    _stack_heads(q_ref, qs_ref, tq)
    m_ref[...] = jnp.full(m_ref.shape, -jnp.inf, F32)
    l_ref[...] = jnp.zeros_like(l_ref)
    acc_ref[...] = jnp.zeros_like(acc_ref)
    nk = vt_ref.shape[0]

    def scores(j, slot):
        k_c = k_ref[pl.ds(pl.multiple_of(j * tk, tk), tk), :]
        s = lax.dot_general(k_c, qs_ref[...], (((1,), (1,)), ((), ())), preferred_element_type=F32)
        s_ref[slot] = s
        cm_ref[slot] = jnp.max(s, axis=0, keepdims=True)

    def update(j, slot):
        m_old = m_ref[...]
        m_new = jnp.maximum(m_old, cm_ref[slot])
        p = jnp.exp2(s_ref[slot] - m_new)
        alpha = jnp.exp2(m_old - m_new)
        l_ref[...] = alpha * l_ref[...] + jnp.sum(p, axis=0, keepdims=True)
        acc_ref[...] = alpha * acc_ref[...] + jnp.dot(vt_ref[j], p.astype(BF16), preferred_element_type=F32)
        m_ref[...] = m_new

    _pipelined(nk, scores, update)
    _store_heads(acc_ref[...] / l_ref[...], o_ref, tq)


def _attention(qkv, norm2, bsz, seq):
    t = qkv.shape[0]
    tk = min(ATT_K_TILE, seq // ATT_SLOTS)
    nk = seq // tk
    assert nk % ATT_SLOTS == 0 and nk * tk == seq, "the score pipeline rotates through ATT_SLOTS key chunks"
    gw = ATT_GROUP * ATT_HD
    v = qkv[:, (ATT_HEADS + ATT_KV_HEADS) * ATT_HD:]
    vt = v.reshape(bsz, nk, tk, ATT_KV_HEADS, ATT_HD).transpose(0, 3, 1, 4, 2)

    tqb = min(ATT_Q_TILE_BOUNDED, seq)
    nqb = seq // tqb
    qn = jnp.sqrt(norm2[:, :ATT_HEADS]).reshape(bsz, nqb, tqb, ATT_KV_HEADS, ATT_GROUP)
    kn = jnp.sqrt(jnp.max(norm2[:, ATT_HEADS:ATT_HEADS + ATT_KV_HEADS].reshape(bsz, seq, ATT_KV_HEADS), axis=1))
    mb = qn.transpose(0, 3, 1, 4, 2) * kn[:, :, None, None, None]
    mb = mb.reshape(bsz * ATT_KV_HEADS * nqb, 1, ATT_GROUP * tqb)

    def call(body, tq, scratch, bound=None):
        nq = seq // tq
        nqs = ATT_GROUP * tq
        in_specs = [
            pl.BlockSpec((tq, gw), lambda b, kv, i: (b * nq + i, kv)),
            pl.BlockSpec((seq, ATT_HD), lambda b, kv, i: (b, ATT_HEADS + kv)),
            pl.BlockSpec((None, None, nk, ATT_HD, tk), lambda b, kv, i: (b, kv, 0, 0, 0)),
        ]
        args = [qkv, qkv, vt]
        if bound is not None:
            in_specs.append(pl.BlockSpec((None, 1, nqs), lambda b, kv, i: ((b * ATT_KV_HEADS + kv) * nq + i, 0, 0)))
            args.append(bound)
        return pl.pallas_call(
            functools.partial(body, tq=tq, tk=tk),
            grid=(bsz, ATT_KV_HEADS, nq),
            in_specs=in_specs,
            out_specs=pl.BlockSpec((tq, gw), lambda b, kv, i: (b * nq + i, kv)),
            out_shape=jax.ShapeDtypeStruct((t, ATT_HEADS * ATT_HD), BF16),
            scratch_shapes=[pltpu.VMEM((nqs, ATT_HD), BF16)] + scratch(nqs),
            compiler_params=_params(),
            name="l1_attention",
        )(*args)

    def bounded():
        return call(_flash_bounded_kernel, tqb,
                    lambda nqs: [pltpu.VMEM((ATT_HD + ATT_ONES_ROWS, nqs), F32), pltpu.VMEM((ATT_SLOTS, tk, nqs), BF16),
                                 pltpu.VMEM((nk, ATT_HD + ATT_ONES_ROWS, tk), BF16)], bound=mb)

    def online():
        return call(_flash_online_kernel, min(ATT_Q_TILE_ONLINE, seq),
                    lambda nqs: [pltpu.VMEM((1, nqs), F32), pltpu.VMEM((1, nqs), F32), pltpu.VMEM((ATT_HD, nqs), F32),
                                 pltpu.VMEM((ATT_SLOTS, tk, nqs), F32), pltpu.VMEM((ATT_SLOTS, 1, nqs), F32)])

    return lax.cond(jnp.max(mb) <= ATT_BOUND_LIMIT, bounded, online)


def _dispatch_kernel(lt_ref, dest_ref, h_ref, xs_hbm, zero_ref, zsem, sem, *, tb, tm):
    @pl.when(pl.program_id(0) == 0)
    def _():
        zero_ref[...] = jnp.zeros_like(zero_ref)
        for e in range(2 * N_EXPERTS):
            fill = pltpu.make_async_copy(zero_ref, xs_hbm.at[pl.ds(lt_ref[e] * tm, tm), :], zsem)
            fill.start()
            fill.wait()

    for r in range(tb):
        for k in range(2):
            pltpu.make_async_copy(h_ref.at[pl.ds(r, 1), :], xs_hbm.at[pl.ds(dest_ref[k, r], 1), :], sem).start()
    for k in range(2):
        pltpu.make_async_copy(h_ref, xs_hbm.at[pl.ds(0, tb), :], sem).wait()


def _dispatch(h, dest, last_tile, n_tiles):
    t, d = h.shape
    tb = DISPATCH_TILE
    tm = MOE_TILE
    return pl.pallas_call(
        functools.partial(_dispatch_kernel, tb=tb, tm=tm),
        grid_spec=pltpu.PrefetchScalarGridSpec(
            num_scalar_prefetch=1, grid=(t // tb,),
            in_specs=[
                pl.BlockSpec((None, 2, tb), lambda i, lt: (i, 0, 0), memory_space=pltpu.SMEM),
                pl.BlockSpec((tb, d), lambda i, lt: (i, 0)),
            ],
            out_specs=pl.BlockSpec(memory_space=pl.ANY),
            scratch_shapes=[pltpu.VMEM((tm, d), F32), pltpu.SemaphoreType.DMA(()), pltpu.SemaphoreType.DMA(())]),
        out_shape=jax.ShapeDtypeStruct((n_tiles * tm, d), F32),
        compiler_params=_params(),
        name="l1_moe_dispatch",
    )(last_tile, dest, h)


def _moe_ffn_kernel(te_ref, nu_ref, x_ref, wg_ref, wu_ref, wd_ref, y_ref, xb_ref, acc_ref):
    i = pl.program_id(0)
    j = pl.program_id(1)
    nj = pl.num_programs(1)
    used = i < nu_ref[0]

    @pl.when(jnp.logical_and(used, j == 0))
    def _():
        xb_ref[...] = x_ref[...].astype(BF16)

    @pl.when(used)
    def _():
        xb = xb_ref[...]
        a = jnp.dot(xb, wg_ref[...], preferred_element_type=F32)
        b = jnp.dot(xb, wu_ref[...], preferred_element_type=F32)
        act = (_silu(a) * b).astype(BF16)
        part = jnp.dot(act, wd_ref[...], preferred_element_type=F32)

        @pl.when(j == 0)
        def _():
            acc_ref[...] = part

        @pl.when(jnp.logical_and(j > 0, j < nj - 1))
        def _():
            acc_ref[...] += part

        @pl.when(j == nj - 1)
        def _():
            y_ref[...] = acc_ref[...] + part

    @pl.when(jnp.logical_and(jnp.logical_not(used), j == nj - 1))
    def _():
        y_ref[...] = jnp.zeros_like(y_ref)


def _moe_ffn(xs, tile_expert, n_used, wg, wu, wd):
    d = xs.shape[1]
    tm = MOE_TILE
    n_tiles = xs.shape[0] // tm
    tf = MOE_FF_TILE
    f = wg.shape[2]
    nj = f // tf
    assert nj >= 2 and nj * tf == f

    def jeff(i, j, nu):
        return jnp.where(i < nu[0], j, nj - 1)

    return pl.pallas_call(
        _moe_ffn_kernel,
        grid_spec=pltpu.PrefetchScalarGridSpec(
            num_scalar_prefetch=2, grid=(n_tiles, nj),
            in_specs=[
                pl.BlockSpec((tm, d), lambda i, j, te, nu: (jnp.minimum(i, nu[0] - 1), 0)),
                pl.BlockSpec((None, d, tf), lambda i, j, te, nu: (te[i], 0, jeff(i, j, nu))),
                pl.BlockSpec((None, d, tf), lambda i, j, te, nu: (te[i], 0, jeff(i, j, nu))),
                pl.BlockSpec((None, tf, d), lambda i, j, te, nu: (te[i], jeff(i, j, nu), 0)),
            ],
            out_specs=pl.BlockSpec((tm, d), lambda i, j, te, nu: (i, 0)),
            scratch_shapes=[pltpu.VMEM((tm, d), BF16), pltpu.VMEM((tm, d), F32)]),
        out_shape=jax.ShapeDtypeStruct((n_tiles * tm, d), F32),
        compiler_params=_params(),
        name="l1_moe_ffn",
    )(tile_expert, n_used, xs, wg, wu, wd)


def _combine_kernel(dfirst_ref, dnext_ref, y_hbm, x_ref, rw_ref, gate_ref, g_ref, sh_ref, sc_ref, o_ref,
                    ya_ref, yb_ref, sem, *, tb):
    i = pl.program_id(0)
    n = pl.num_programs(0)
    slot = i % 2

    def row_copies(dest_ref, r, buf_slot):
        ca = pltpu.make_async_copy(y_hbm.at[pl.ds(dest_ref[0, r], 1), :], ya_ref.at[buf_slot, pl.ds(r, 1), :],
                                   sem.at[buf_slot])
        cb = pltpu.make_async_copy(y_hbm.at[pl.ds(dest_ref[1, r], 1), :], yb_ref.at[buf_slot, pl.ds(r, 1), :],
                                   sem.at[buf_slot])
        return ca, cb

    @pl.when(i == 0)
    def _():
        def issue(r, c):
            ca, cb = row_copies(dfirst_ref, r, 0)
            ca.start()
            cb.start()
            return c
        lax.fori_loop(0, tb, issue, 0)

    @pl.when(i + 1 < n)
    def _():
        for r in range(tb):
            ca, cb = row_copies(dnext_ref, r, 1 - slot)
            ca.start()
            cb.start()

    pltpu.make_async_copy(y_hbm.at[pl.ds(0, tb), :], ya_ref.at[slot], sem.at[slot]).wait()
    pltpu.make_async_copy(y_hbm.at[pl.ds(0, tb), :], yb_ref.at[slot], sem.at[slot]).wait()
    rw = rw_ref[...]
    f = rw[:, 0:1] * ya_ref[slot] + rw[:, 1:2] * yb_ref[slot]
    x = x_ref[...] + gate_ref[...] * f
    o_ref[...] = _rms_mod(x, g_ref[...], sh_ref[...], sc_ref[...])


def _combine(dest, ys, x, rw, gate, g, sh, sc, seq):
    t, d = x.shape
    tb = COMBINE_TILE
    per_b = seq // tb
    n = t // tb
    vec = lambda: pl.BlockSpec((1, d), lambda i: (0, 0))
    bvec = lambda: pl.BlockSpec((None, 1, d), lambda i: (i // per_b, 0, 0))
    return pl.pallas_call(
        functools.partial(_combine_kernel, tb=tb),
        grid=(n,),
        in_specs=[
            pl.BlockSpec((None, 2, tb), lambda i: (0, 0, 0), memory_space=pltpu.SMEM),
            pl.BlockSpec((None, 2, tb), lambda i: (jnp.minimum(i + 1, n - 1), 0, 0), memory_space=pltpu.SMEM),
            pl.BlockSpec(memory_space=pl.ANY),
            pl.BlockSpec((tb, d), lambda i: (i, 0)),
            pl.BlockSpec((tb, LANES), lambda i: (i, 0)),
            bvec(), vec(), bvec(), bvec(),
        ],
        out_specs=pl.BlockSpec((tb, d), lambda i: (i, 0)),
        out_shape=jax.ShapeDtypeStruct((t, d), F32),
        scratch_shapes=[pltpu.VMEM((2, tb, d), F32), pltpu.VMEM((2, tb, d), F32), pltpu.SemaphoreType.DMA((2,))],
        compiler_params=_params(),
        name="l1_moe_combine",
    )(dest, dest, ys, x, rw, gate, g.reshape(1, d), sh, sc)


def _moe_plan(idx, t):
    tm = MOE_TILE
    n_tiles = (2 * t) // tm + N_EXPERTS
    e_flat = idx.reshape(-1)
    experts = jnp.arange(N_EXPERTS, dtype=jnp.int32)
    onehot = (e_flat[:, None] == experts[None, :]).astype(jnp.int32)
    incl = jnp.cumsum(onehot, axis=0)
    counts = incl[-1]
    pos = jnp.sum((incl - onehot) * onehot, axis=1)
    tiles_per_e = (counts + tm - 1) // tm
    tile_end = jnp.cumsum(tiles_per_e)
    row_start = (tile_end - tiles_per_e) * tm
    dest = (row_start[e_flat] + pos).astype(jnp.int32)
    n_used = tile_end[-1:]
    tile_ids = jnp.arange(n_tiles, dtype=jnp.int32)
    tile_expert = jnp.minimum(jnp.sum((tile_ids[:, None] >= tile_end[None, :]).astype(jnp.int32), axis=1),
                              N_EXPERTS - 1)
    last_e = jnp.max(jnp.where(counts > 0, experts, 0))
    tile_expert = jnp.where(tile_ids < n_used[0], tile_expert, last_e).astype(jnp.int32)
    last_tile = jnp.maximum(tile_end - 1, 0)
    unused = jnp.minimum(n_used[0] + experts, n_tiles - 1)
    clear_tiles = jnp.concatenate([last_tile, unused]).astype(jnp.int32)
    return dest, tile_expert, n_used.astype(jnp.int32), clear_tiles, n_tiles


def _by_tile(dest, t, tb):
    return dest.reshape(t // tb, tb, 2).transpose(0, 2, 1)


def kernel(x, c, l0_ada_w, l0_ada_b, l0_norm1_g, l0_norm2_g, l0_ret_w_in, l0_ret_decay_logit, l0_ret_gn_g, l0_ret_w_out, l0_ffn_w_gate, l0_ffn_w_up, l0_ffn_w_down, l1_ada_w, l1_ada_b, l1_norm1_g, l1_norm2_g, l1_attn_w_qkv, l1_attn_q_norm_g, l1_attn_k_norm_g, l1_attn_w_out, l1_moe_w_router, l1_moe_b_router, l1_moe_w_gate, l1_moe_w_up, l1_moe_w_down, final_ada_w, final_ada_b, final_norm_g):
    bsz, seq, d = x.shape
    t = bsz * seq
    xf = x.reshape(t, d)

    c8 = jnp.zeros((8, d), F32).at[:bsz].set(c)

    def mods(w, b, n):
        m = _adaln(c8, w, b)[:bsz]
        return [m[:, i * d:(i + 1) * d].reshape(bsz, 1, d) for i in range(n)]

    sh1, sc1, g1, sh2, sc2, g2 = mods(l0_ada_w, l0_ada_b, 6)
    sh3, sc3, g3, sh4, sc4, g4 = mods(l1_ada_w, l1_ada_b, 6)
    fsh, fsc = mods(final_ada_w, final_ada_b, 2)

    bf = lambda w: w.astype(BF16)

    proj = _inproj0(xf, l0_norm1_g, sh1, sc1, bf(l0_ret_w_in), seq)
    log_gamma = jax.nn.log_sigmoid(l0_ret_decay_logit.astype(F32))
    u = _retention(proj, log_gamma, l0_ret_gn_g, bsz, seq)
    x1, h2 = _outproj(u, bf(l0_ret_w_out), xf, g1, l0_norm2_g, sh2, sc2, seq)
    x2, h3 = _ffn0(h2, bf(l0_ffn_w_gate), bf(l0_ffn_w_up), bf(l0_ffn_w_down), x1, g2, l1_norm1_g, sh3, sc3, seq)

    qkv, norm2 = _qkv1(h3, bf(l1_attn_w_qkv), l1_attn_q_norm_g, l1_attn_k_norm_g, seq)
    o = _attention(qkv, norm2, bsz, seq)
    wr = jnp.zeros((d, LANES), F32).at[:, :N_EXPERTS].set(l1_moe_w_router)
    wr_hi = wr.astype(BF16)
    wr_lo = (wr - wr_hi.astype(F32)).astype(BF16)
    br = jnp.full((1, LANES), -jnp.inf, F32).at[0, :N_EXPERTS].set(l1_moe_b_router.astype(F32))
    x3, h4, idx, rw = _outproj(o, bf(l1_attn_w_out), x2, g3, l1_norm2_g, sh4, sc4, seq,
                               router=(jnp.concatenate([wr_hi, wr_lo], axis=1), br), h_dtype=F32)
    dest, tile_expert, n_used, clear_tiles, n_tiles = _moe_plan(idx[:, :2], t)
    xs = _dispatch(h4, _by_tile(dest, t, DISPATCH_TILE), clear_tiles, n_tiles)
    ys = _moe_ffn(xs, tile_expert, n_used, bf(l1_moe_w_gate), bf(l1_moe_w_up), bf(l1_moe_w_down))
    out = _combine(_by_tile(dest, t, COMBINE_TILE), ys, x3, rw, g4, final_norm_g, fsh, fsc, seq)
    return out.reshape(bsz, seq, d)
```

```python
import functools

import numpy as np
import jax
import jax.numpy as jnp
from jax import lax
from jax.experimental import pallas as pl
from jax.experimental.pallas import tpu as pltpu

GRID_W = 64
NORM_EPS = 1e-6
ROPE_THETA = 10000.0
RET_HEADS = 4
RET_DK = 256
RET_DV = 512
RET_CHUNK = 128
ATT_HEADS = 8
ATT_KV_HEADS = 2
ATT_HD = 128
ATT_GROUP = ATT_HEADS // ATT_KV_HEADS
N_EXPERTS = 8
LANES = 128
LOG2_E = 1.4426950408889634

F32 = jnp.float32
BF16 = jnp.bfloat16
HIGHEST = lax.Precision.HIGHEST

TOKEN_TILE = 512
RET_BLOCK = 1024
ATT_Q_TILE_BOUNDED = 512
ATT_Q_TILE_ONLINE = 256
ATT_K_TILE = 1024
ATT_SLOTS = 4
ATT_ONES_ROWS = 16
ATT_BOUND_LIMIT = 60.0
MOE_TILE = 512
MOE_FF_TILE = 1792
DISPATCH_TILE = 512
COMBINE_TILE = 256
VMEM_LIMIT = 56 * 1024 * 1024


def _params(**kw):
    return pltpu.CompilerParams(vmem_limit_bytes=VMEM_LIMIT, **kw)


def _resident(shape):
    return pl.BlockSpec(shape, lambda *_: (0,) * len(shape), pipeline_mode=pl.Buffered(1))


def _silu(a):
    return a * jax.nn.sigmoid(a)


def _rms_mod(x, g, shift, scale):
    y = x * lax.rsqrt(jnp.mean(x * x, axis=-1, keepdims=True) + NORM_EPS)
    return (y * g) * (1.0 + scale) + shift


def _adaln_kernel(c_ref, w_ref, b_ref, o_ref):
    a = _silu(c_ref[...])
    o_ref[...] = jnp.dot(a, w_ref[...], precision=HIGHEST, preferred_element_type=F32) + b_ref[...]


def _adaln(c8, w, b):
    d, n = w.shape
    tn = 1024
    return pl.pallas_call(
        _adaln_kernel,
        grid=(n // tn,),
        in_specs=[
            pl.BlockSpec((8, d), lambda j: (0, 0)),
            pl.BlockSpec((d, tn), lambda j: (0, j)),
            pl.BlockSpec((1, tn), lambda j: (0, j)),
        ],
        out_specs=pl.BlockSpec((8, tn), lambda j: (0, j)),
        out_shape=jax.ShapeDtypeStruct((8, n), F32),
        compiler_params=_params(),
        name="adaln",
    )(c8, w, b.reshape(1, n))


def _rope_tables(seq, half):
    nfreq = half // 2
    freqs = ROPE_THETA ** (-np.arange(0, nfreq, dtype=np.float64) * 2.0 / half)
    npos = max(seq // GRID_W, GRID_W)
    ang = np.arange(npos, dtype=np.float64)[:, None] * freqs[None, :]
    cos = np.concatenate([np.cos(ang), np.cos(ang)], axis=1)
    sin = np.concatenate([-np.sin(ang), np.sin(ang)], axis=1)
    return cos.astype(np.float32), sin.astype(np.float32)


def _inproj0_kernel(x_ref, g_ref, sh_ref, sc_ref, w_ref, cr_ref, sr_ref, cc_ref, scol_ref, o_ref, *, tm):
    h = _rms_mod(x_ref[...], g_ref[...], sh_ref[...], sc_ref[...]).astype(BF16)
    rows = tm // GRID_W
    cr = cr_ref[...]
    sr = sr_ref[...]
    cc = cc_ref[...][None]
    scol = scol_ref[...][None]
    nw = 512
    n_qk = 2 * RET_HEADS * RET_DK
    for c in range(o_ref.shape[1] // nw):
        y = jnp.dot(h, w_ref[:, c * nw:(c + 1) * nw], preferred_element_type=F32)
        if c * nw < n_qk:
            is_k = c * nw >= RET_HEADS * RET_DK
            for s in range(nw // LANES):
                ys = y[:, s * LANES:(s + 1) * LANES]
                y3 = ys.reshape(rows, GRID_W, LANES)
                r3 = pltpu.roll(ys, LANES // 2, 1).reshape(rows, GRID_W, LANES)
                if s % 2 == 0:
                    o3 = y3 * cr + r3 * sr
                else:
                    o3 = y3 * cc + r3 * scol
                if is_k:
                    o3 = o3 * (RET_DK ** -0.5)
                o_ref[:, c * nw + s * LANES:c * nw + (s + 1) * LANES] = o3.reshape(tm, LANES).astype(BF16)
        else:
            o_ref[:, c * nw:(c + 1) * nw] = y.astype(BF16)


def _inproj0(x, g, sh, sc, w_bf16, seq):
    t, d = x.shape
    n = w_bf16.shape[1]
    tm = TOKEN_TILE
    per_b = seq // tm
    rows = tm // GRID_W
    cos, sin = _rope_tables(seq, LANES)
    n_img_rows = seq // GRID_W
    cr = jnp.asarray(cos[:n_img_rows].reshape(n_img_rows, 1, LANES))
    sr = jnp.asarray(sin[:n_img_rows].reshape(n_img_rows, 1, LANES))
    cc = jnp.asarray(cos[:GRID_W])
    scol = jnp.asarray(sin[:GRID_W])
    vec = lambda: pl.BlockSpec((1, d), lambda i: (0, 0))
    bvec = lambda: pl.BlockSpec((None, 1, d), lambda i: (i // per_b, 0, 0))
    rowt = lambda: pl.BlockSpec((rows, 1, LANES), lambda i: (i % per_b, 0, 0))
    colt = lambda: pl.BlockSpec((GRID_W, LANES), lambda i: (0, 0))
    return pl.pallas_call(
        functools.partial(_inproj0_kernel, tm=tm),
        grid=(t // tm,),
        in_specs=[pl.BlockSpec((tm, d), lambda i: (i, 0)), vec(), bvec(), bvec(),
                  _resident((d, n)), rowt(), rowt(), colt(), colt()],
        out_specs=pl.BlockSpec((tm, n), lambda i: (i, 0)),
        out_shape=jax.ShapeDtypeStruct((t, n), BF16),
        compiler_params=_params(),
        name="l0_inproj",
    )(x, g.reshape(1, d), sh, sc, w_bf16, cr, sr, cc, scol)


def _col(n):
    return lax.broadcasted_iota(jnp.int32, (n, 1), 0).astype(F32)


def _ret_fwd_kernel(lg_ref, q_ref, k_ref, v_ref, y_ref, state_ref, *, nchunks):
    @pl.when(pl.program_id(1) == 0)
    def _():
        state_ref[...] = jnp.zeros_like(state_ref)

    c_len = RET_CHUNK
    ii = lax.broadcasted_iota(jnp.int32, (c_len, c_len), 0)
    jj = lax.broadcasted_iota(jnp.int32, (c_len, c_len), 1)
    diff = (ii - jj).astype(F32)
    idx = _col(c_len)
    consts = []
    for hd in range(RET_HEADS):
        lgf = lg_ref[0, hd]
        lgb = lg_ref[1, hd]
        dmat = jnp.where(diff >= 0, jnp.exp(diff * lgf), jnp.exp(-diff * lgb))
        xi = jnp.exp((idx + 1.0) * lgf)
        zeta = jnp.exp((c_len - 1.0 - idx) * lgf)
        decay = jnp.exp(jnp.full((1, 1), c_len, F32) * lgf)
        consts.append((dmat, xi, zeta, decay))
    for c in range(nchunks):
        sl = slice(c * c_len, (c + 1) * c_len)
        for hd in range(RET_HEADS):
            dmat, xi, zeta, decay = consts[hd]
            qk = slice(hd * RET_DK, (hd + 1) * RET_DK)
            vv = slice(hd * RET_DV, (hd + 1) * RET_DV)
            q = q_ref[sl, qk]
            k = k_ref[sl, qk]
            v = v_ref[sl, vv]
            s = lax.dot_general(q, k, (((1,), (1,)), ((), ())), preferred_element_type=F32) * dmat
            y = jnp.dot(s.astype(BF16), v, preferred_element_type=F32)
            st = state_ref[hd]
            y = y + jnp.dot(q, st.astype(BF16), preferred_element_type=F32) * xi
            kz = (k.astype(F32) * zeta).astype(BF16)
            upd = lax.dot_general(kz, v, (((0,), (0,)), ((), ())), preferred_element_type=F32)
            state_ref[hd] = st * decay + upd
            y_ref[sl, vv] = y.astype(y_ref.dtype)


def _ret_bwd_kernel(lg_ref, q_ref, k_ref, v_ref, g_ref, yf_ref, gn_ref, u_ref, state_ref, *, nchunks):
    @pl.when(pl.program_id(1) == 0)
    def _():
        state_ref[...] = jnp.zeros_like(state_ref)

    c_len = RET_CHUNK
    idx = _col(c_len)
    consts = []
    for hd in range(RET_HEADS):
        lgb = lg_ref[1, hd]
        consts.append((jnp.exp((c_len - idx) * lgb), jnp.exp(idx * lgb),
                       jnp.exp(jnp.full((1, 1), c_len, F32) * lgb)))
    for c in reversed(range(nchunks)):
        sl = slice(c * c_len, (c + 1) * c_len)
        for hd in range(RET_HEADS):
            xi, zeta, decay = consts[hd]
            qk = slice(hd * RET_DK, (hd + 1) * RET_DK)
            vv = slice(hd * RET_DV, (hd + 1) * RET_DV)
            q = q_ref[sl, qk]
            k = k_ref[sl, qk]
            v = v_ref[sl, vv]
            st = state_ref[hd]
            y = yf_ref[sl, vv].astype(F32) + jnp.dot(q, st.astype(BF16), preferred_element_type=F32) * xi
            kz = (k.astype(F32) * zeta).astype(BF16)
            upd = lax.dot_general(kz, v, (((0,), (0,)), ((), ())), preferred_element_type=F32)
            state_ref[hd] = st * decay + upd
            mu = jnp.mean(y, axis=-1, keepdims=True)
            yc = y - mu
            var = jnp.mean(yc * yc, axis=-1, keepdims=True)
            yn = yc * lax.rsqrt(var + NORM_EPS) * gn_ref[:, vv]
            u_ref[sl, vv] = (_silu(g_ref[sl, vv].astype(F32)) * yn).astype(u_ref.dtype)


def _retention(proj, log_gamma, gn_g, bsz, seq):
    t = proj.shape[0]
    tb = RET_BLOCK
    nb = seq // tb
    nchunks = tb // RET_CHUNK
    n_qk = RET_HEADS * RET_DK
    n_v = RET_HEADS * RET_DV
    assert n_v == 2 * n_qk
    grid = (bsz, nb)
    qspec = lambda f: pl.BlockSpec((tb, n_qk), lambda b, n, lg: (b * nb + f(n), 0))
    kspec = lambda f: pl.BlockSpec((tb, n_qk), lambda b, n, lg: (b * nb + f(n), 1))
    vspec = lambda f: pl.BlockSpec((tb, n_v), lambda b, n, lg: (b * nb + f(n), 1))
    gspec = lambda f: pl.BlockSpec((tb, n_v), lambda b, n, lg: (b * nb + f(n), 2))
    yspec = lambda f: pl.BlockSpec((tb, n_v), lambda b, n, lg: (b * nb + f(n), 0))
    fwd = lambda n: n
    rev = lambda n: nb - 1 - n
    state = pltpu.VMEM((RET_HEADS, RET_DK, RET_DV), F32)
    yf = pl.pallas_call(
        functools.partial(_ret_fwd_kernel, nchunks=nchunks),
        grid_spec=pltpu.PrefetchScalarGridSpec(
            num_scalar_prefetch=1, grid=grid,
            in_specs=[qspec(fwd), kspec(fwd), vspec(fwd)],
            out_specs=yspec(fwd), scratch_shapes=[state]),
        out_shape=jax.ShapeDtypeStruct((t, RET_HEADS * RET_DV), BF16),
        compiler_params=_params(),
        name="l0_ret_fwd",
    )(log_gamma, proj, proj, proj)
    return pl.pallas_call(
        functools.partial(_ret_bwd_kernel, nchunks=nchunks),
        grid_spec=pltpu.PrefetchScalarGridSpec(
            num_scalar_prefetch=1, grid=grid,
            in_specs=[qspec(rev), kspec(rev), vspec(rev), gspec(rev), yspec(rev),
                      pl.BlockSpec((1, n_v), lambda b, n, lg: (0, 0))],
            out_specs=yspec(rev), scratch_shapes=[state]),
        out_shape=jax.ShapeDtypeStruct((t, RET_HEADS * RET_DV), BF16),
        compiler_params=_params(),
        name="l0_ret_bwd",
    )(log_gamma, proj, proj, proj, proj, yf, gn_g.reshape(1, -1))


def _top2(logits):
    lane = lax.broadcasted_iota(jnp.int32, logits.shape, 1)
    v1 = jnp.max(logits, axis=-1, keepdims=True)
    i1 = jnp.min(jnp.where(logits == v1, lane, LANES), axis=-1, keepdims=True)
    rest = jnp.where(lane == i1, -jnp.inf, logits)
    v2 = jnp.max(rest, axis=-1, keepdims=True)
    i2 = jnp.min(jnp.where(rest == v2, lane, LANES), axis=-1, keepdims=True)
    e = jnp.exp(v2 - v1)
    w1 = 1.0 / (1.0 + e)
    w2 = e / (1.0 + e)
    idx = jnp.where(lane == 0, i1, jnp.where(lane == 1, i2, 0))
    w = jnp.where(lane == 0, w1, jnp.where(lane == 1, w2, 0.0))
    return idx, w


def _outproj_kernel(u_ref, w_ref, x_ref, gate_ref, g_ref, sh_ref, sc_ref, *rest, router):
    if router:
        wr_ref, br_ref, xo_ref, ho_ref, idx_ref, rw_ref = rest
    else:
        xo_ref, ho_ref = rest
    m = jnp.dot(u_ref[...], w_ref[...], preferred_element_type=F32)
    x1 = x_ref[...] + gate_ref[...] * m
    xo_ref[...] = x1
    h = _rms_mod(x1, g_ref[...], sh_ref[...], sc_ref[...])
    ho_ref[...] = h.astype(ho_ref.dtype)
    if router:
        h_hi = h.astype(BF16)
        h_lo = (h - h_hi.astype(F32)).astype(BF16)
        a = jnp.dot(h_hi, wr_ref[...], preferred_element_type=F32)
        b = jnp.dot(h_lo, wr_ref[:, :LANES], preferred_element_type=F32)
        logits = (a[:, :LANES] + a[:, LANES:]) + b + br_ref[...]
        idx, w = _top2(logits)
        idx_ref[...] = idx
        rw_ref[...] = w


def _outproj(u, w_bf16, x, gate, g, sh, sc, seq, router=None, h_dtype=BF16):
    t, d = x.shape
    kdim = u.shape[1]
    tm = TOKEN_TILE
    per_b = seq // tm
    row = lambda width: pl.BlockSpec((tm, width), lambda i: (i, 0))
    vec = lambda: pl.BlockSpec((1, d), lambda i: (0, 0))
    bvec = lambda: pl.BlockSpec((None, 1, d), lambda i: (i // per_b, 0, 0))
    in_specs = [row(kdim), _resident((kdim, d)), row(d), bvec(), vec(), bvec(), bvec()]
    args = [u, w_bf16, x, gate, g.reshape(1, d), sh, sc]
    out_specs = [row(d), row(d)]
    out_shape = [jax.ShapeDtypeStruct((t, d), F32), jax.ShapeDtypeStruct((t, d), h_dtype)]
    if router is not None:
        wr, br = router
        in_specs += [_resident((d, 2 * LANES)), pl.BlockSpec((1, LANES), lambda i: (0, 0))]
        args += [wr, br]
        out_specs += [row(LANES), row(LANES)]
        out_shape += [jax.ShapeDtypeStruct((t, LANES), jnp.int32), jax.ShapeDtypeStruct((t, LANES), F32)]
    return pl.pallas_call(
        functools.partial(_outproj_kernel, router=router is not None),
        grid=(t // tm,),
        in_specs=in_specs, out_specs=out_specs, out_shape=out_shape,
        compiler_params=_params(),
        name="outproj_router" if router is not None else "outproj",
    )(*args)


def _ffn0_kernel(h_ref, wg_ref, wu_ref, wd_ref, x_ref, gate_ref, g_ref, sh_ref, sc_ref, xo_ref, ho_ref, *, tf):
    h = h_ref[...]
    acc = jnp.zeros(x_ref.shape, F32)
    for c in range(wg_ref.shape[1] // tf):
        cs = slice(c * tf, (c + 1) * tf)
        a = jnp.dot(h, wg_ref[:, cs], preferred_element_type=F32)
        b = jnp.dot(h, wu_ref[:, cs], preferred_element_type=F32)
        act = (_silu(a) * b).astype(BF16)
        acc = acc + jnp.dot(act, wd_ref[cs, :], preferred_element_type=F32)
    x2 = x_ref[...] + gate_ref[...] * acc
    xo_ref[...] = x2
    ho_ref[...] = _rms_mod(x2, g_ref[...], sh_ref[...], sc_ref[...]).astype(ho_ref.dtype)


def _ffn0(h, wg, wu, wd, x, gate, g, sh, sc, seq):
    t, d = x.shape
    f = wg.shape[1]
    tm = TOKEN_TILE
    per_b = seq // tm
    row = lambda: pl.BlockSpec((tm, d), lambda i: (i, 0))
    vec = lambda: pl.BlockSpec((1, d), lambda i: (0, 0))
    bvec = lambda: pl.BlockSpec((None, 1, d), lambda i: (i // per_b, 0, 0))
    return pl.pallas_call(
        functools.partial(_ffn0_kernel, tf=f),
        grid=(t // tm,),
        in_specs=[row(), _resident((d, f)), _resident((d, f)), _resident((f, d)), row(), bvec(), vec(), bvec(), bvec()],
        out_specs=[row(), row()],
        out_shape=[jax.ShapeDtypeStruct((t, d), F32), jax.ShapeDtypeStruct((t, d), BF16)],
        compiler_params=_params(),
        name="l0_ffn",
    )(h, wg, wu, wd, x, gate, g.reshape(1, d), sh, sc)


def _qkv1_kernel(h_ref, w_ref, qg_ref, kg_ref, cr_ref, s1r_ref, s2r_ref, cc_ref, s1c_ref, s2c_ref, o_ref, n2_ref,
                 *, tm):
    rows = tm // GRID_W
    y = jnp.dot(h_ref[...], w_ref[...], preferred_element_type=F32)
    cos =(cr_ref[...] + cc_ref[...][None]).reshape(tm, LANES)
    s1 = (s1r_ref[...] + s1c_ref[...][None]).reshape(tm, LANES)
    s2 = (s2r_ref[...] + s2c_ref[...][None]).reshape(tm, LANES)
    n_rot = ATT_HEADS + ATT_KV_HEADS
    lane = lax.broadcasted_iota(jnp.int32, (tm, LANES), 1)
    norm2 = jnp.zeros((tm, LANES), F32)
    for j in range(n_rot):
        ys = y[:, j * LANES:(j + 1) * LANES]
        gain = qg_ref[...] if j < ATT_HEADS else kg_ref[...]
        yn = ys * lax.rsqrt(jnp.mean(ys * ys, axis=-1, keepdims=True) + NORM_EPS) * gain
        o = yn * cos + pltpu.roll(yn, 96, 1) * s1 + pltpu.roll(yn, 32, 1) * s2
        if j < ATT_HEADS:
            o = o * (ATT_HD ** -0.5 * LOG2_E)
        ob = o.astype(BF16)
        o_ref[:, j * LANES:(j + 1) * LANES] = ob
        obf = ob.astype(F32)
        norm2 = jnp.where(lane == j, jnp.sum(obf * obf, axis=-1, keepdims=True), norm2)
    o_ref[:, n_rot * LANES:] = y[:, n_rot * LANES:].astype(BF16)
    n2_ref[...] = norm2


def _att_rope_tables(seq):
    half = ATT_HD // 2
    nfreq = half // 2
    freqs = ROPE_THETA ** (-np.arange(0, nfreq, dtype=np.float64) * 2.0 / half)
    lane = np.arange(LANES)
    f_l = freqs[lane % nfreq]
    first = (lane % half) < nfreq
    is_row = lane < half
    n_img_rows = seq // GRID_W

    def tables(npos, mask):
        ang = np.arange(npos, dtype=np.float64)[:, None] * f_l[None, :]
        c = np.where(mask[None], np.cos(ang), 0.0)
        s1 = np.where((mask & first)[None], -np.sin(ang), 0.0)
        s2 = np.where((mask & ~first)[None], np.sin(ang), 0.0)
        return [a.astype(np.float32) for a in (c, s1, s2)]

    row_t = [jnp.asarray(a.reshape(n_img_rows, 1, LANES)) for a in tables(n_img_rows, is_row)]
    col_t = [jnp.asarray(a) for a in tables(GRID_W, ~is_row)]
    return row_t, col_t


def _qkv1(h, w_bf16, qg, kg, seq):
    t, d = h.shape
    n = w_bf16.shape[1]
    tm = TOKEN_TILE
    per_b = seq // tm
    rows = tm // GRID_W
    row_t, col_t = _att_rope_tables(seq)
    rowt = lambda: pl.BlockSpec((rows, 1, LANES), lambda i: (i % per_b, 0, 0))
    colt = lambda: pl.BlockSpec((GRID_W, LANES), lambda i: (0, 0))
    hvec = lambda: pl.BlockSpec((1, LANES), lambda i: (0, 0))
    return pl.pallas_call(
        functools.partial(_qkv1_kernel, tm=tm),
        grid=(t // tm,),
        in_specs=[pl.BlockSpec((tm, d), lambda i: (i, 0)), _resident((d, n)), hvec(), hvec(),
                  rowt(), rowt(), rowt(), colt(), colt(), colt()],
        out_specs=[pl.BlockSpec((tm, n), lambda i: (i, 0)), pl.BlockSpec((tm, LANES), lambda i: (i, 0))],
        out_shape=[jax.ShapeDtypeStruct((t, n), BF16), jax.ShapeDtypeStruct((t, LANES), F32)],
        compiler_params=_params(),
        name="l1_qkv",
    )(h, w_bf16, qg.reshape(1, LANES), kg.reshape(1, LANES), *row_t, *col_t)


def _stack_heads(q_ref, qs_ref, tq):
    for g in range(ATT_GROUP):
        qs_ref[g * tq:(g + 1) * tq, :] = q_ref[:, g * LANES:(g + 1) * LANES]


def _pipelined(nk, scores, update):
    scores(0, 0)
    scores(1, 1)

    def body(jj, carry):
        j = ATT_SLOTS * jj
        for u in range(ATT_SLOTS):
            scores(j + u + 2, (u + 2) % ATT_SLOTS)
            update(j + u, u)
        return carry

    lax.fori_loop(0, nk // ATT_SLOTS - 1, body, 0)
    j = nk - ATT_SLOTS
    for u in range(ATT_SLOTS):
        if u + 2 < ATT_SLOTS:
            scores(j + u + 2, (u + 2) % ATT_SLOTS)
        update(j + u, u)


def _store_heads(o, o_ref, tq):
    for g in range(ATT_GROUP):
        o_ref[:, g * LANES:(g + 1) * LANES] = o[:, g * tq:(g + 1) * tq].T.astype(o_ref.dtype)


def _flash_bounded_kernel(q_ref, k_ref, vt_ref, mb_ref, o_ref, qs_ref, acc_ref, p_ref, vte_ref, *, tq, tk):
    nk = vt_ref.shape[0]

    @pl.when(pl.program_id(2) == 0)
    def _():
        for j in range(nk):
            vte_ref[j, :ATT_HD, :] = vt_ref[j]
            vte_ref[j, ATT_HD:, :] = jnp.ones((ATT_ONES_ROWS, tk), BF16)

    _stack_heads(q_ref, qs_ref, tq)
    acc_ref[...] = jnp.zeros_like(acc_ref)
    mb = mb_ref[...]

    def scores(j, slot):
        k_c = k_ref[pl.ds(pl.multiple_of(j * tk, tk), tk), :]
        s = lax.dot_general(k_c, qs_ref[...], (((1,), (1,)), ((), ())), preferred_element_type=F32)
        p_ref[slot] = jnp.exp2(s - mb).astype(BF16)

    def update(j, slot):
        acc_ref[...] += jnp.dot(vte_ref[j], p_ref[slot], preferred_element_type=F32)

    _pipelined(nk, scores, update)
    _store_heads(acc_ref[:ATT_HD, :] / acc_ref[ATT_HD:ATT_HD + 1, :], o_ref, tq)


def _flash_online_kernel(q_ref, k_ref, vt_ref, o_ref, qs_ref, m_ref, l_ref, acc_ref, s_ref, cm_ref, *, tq, tk):
    _stack_heads(q_ref, qs_ref, tq)
    m_ref[...] = jnp.full(m_ref.shape, -jnp.inf, F32)
    l_ref[...] = jnp.zeros_like(l_ref)
    acc_ref[...] = jnp.zeros_like(acc_ref)
    nk = vt_ref.shape[0]

    def scores(j, slot):
        k_c = k_ref[pl.ds(pl.multiple_of(j * tk, tk), tk), :]
        s = lax.dot_general(k_c, qs_ref[...], (((1,), (1,)), ((), ())), preferred_element_type=F32)
        s_ref[slot] = s
        cm_ref[slot] = jnp.max(s, axis=0, keepdims=True)

    def update(j, slot):
        m_old = m_ref[...]
        m_new = jnp.maximum(m_old, cm_ref[slot])
        p = jnp.exp2(s_ref[slot] - m_new)
        alpha = jnp.exp2(m_old - m_new)
        l_ref[...] = alpha * l_ref[...] + jnp.sum(p, axis=0, keepdims=True)
        acc_ref[...] = alpha * acc_ref[...] + jnp.dot(vt_ref[j], p.astype(BF16), preferred_element_type=F32)
        m_ref[...] = m_new

    _pipelined(nk, scores, update)
    _store_heads(acc_ref[...] / l_ref[...], o_ref, tq)


def _attention(qkv, norm2, bsz, seq):
    t = qkv.shape[0]
    tk = min(ATT_K_TILE, seq // ATT_SLOTS)
    nk = seq // tk
    assert nk % ATT_SLOTS == 0 and nk * tk == seq, "the score pipeline rotates through ATT_SLOTS key chunks"
    gw = ATT_GROUP * ATT_HD
    v = qkv[:, (ATT_HEADS + ATT_KV_HEADS) * ATT_HD:]
    vt = v.reshape(bsz, nk, tk, ATT_KV_HEADS, ATT_HD).transpose(0, 3, 1, 4, 2)

    tqb = min(ATT_Q_TILE_BOUNDED, seq)
    nqb = seq // tqb
    qn = jnp.sqrt(norm2[:, :ATT_HEADS]).reshape(bsz, nqb, tqb, ATT_KV_HEADS, ATT_GROUP)
    kn = jnp.sqrt(jnp.max(norm2[:, ATT_HEADS:ATT_HEADS + ATT_KV_HEADS].reshape(bsz, seq, ATT_KV_HEADS), axis=1))
    mb = qn.transpose(0, 3, 1, 4, 2) * kn[:, :, None, None, None]
    mb = mb.reshape(bsz * ATT_KV_HEADS * nqb, 1, ATT_GROUP * tqb)

    def call(body, tq, scratch, bound=None):
        nq = seq // tq
        nqs = ATT_GROUP * tq
        in_specs = [
            pl.BlockSpec((tq, gw), lambda b, kv, i: (b * nq + i, kv)),
            pl.BlockSpec((seq, ATT_HD), lambda b, kv, i: (b, ATT_HEADS + kv)),
            pl.BlockSpec((None, None, nk, ATT_HD, tk), lambda b, kv, i: (b, kv, 0, 0, 0)),
        ]
        args = [qkv, qkv, vt]
        if bound is not None:
            in_specs.append(pl.BlockSpec((None, 1, nqs), lambda b, kv, i: ((b * ATT_KV_HEADS + kv) * nq + i, 0, 0)))
            args.append(bound)
        return pl.pallas_call(
            functools.partial(body, tq=tq, tk=tk),
            grid=(bsz, ATT_KV_HEADS, nq),
            in_specs=in_specs,
            out_specs=pl.BlockSpec((tq, gw), lambda b, kv, i: (b * nq + i, kv)),
            out_shape=jax.ShapeDtypeStruct((t, ATT_HEADS * ATT_HD), BF16),
            scratch_shapes=[pltpu.VMEM((nqs, ATT_HD), BF16)] + scratch(nqs),
            compiler_params=_params(),
            name="l1_attention",
        )(*args)

    def bounded():
        return call(_flash_bounded_kernel, tqb,
                    lambda nqs: [pltpu.VMEM((ATT_HD + ATT_ONES_ROWS, nqs), F32), pltpu.VMEM((ATT_SLOTS, tk, nqs), BF16),
                                 pltpu.VMEM((nk, ATT_HD + ATT_ONES_ROWS, tk), BF16)], bound=mb)

    def online():
        return call(_flash_online_kernel, min(ATT_Q_TILE_ONLINE, seq),
                    lambda nqs: [pltpu.VMEM((1, nqs), F32), pltpu.VMEM((1, nqs), F32), pltpu.VMEM((ATT_HD, nqs), F32),
                                 pltpu.VMEM((ATT_SLOTS, tk, nqs), F32), pltpu.VMEM((ATT_SLOTS, 1, nqs), F32)])

    return lax.cond(jnp.max(mb) <= ATT_BOUND_LIMIT, bounded, online)


def _dispatch_kernel(lt_ref, dest_ref, h_ref, xs_hbm, zero_ref, zsem, sem, *, tb, tm):
    @pl.when(pl.program_id(0) == 0)
    def _():
        zero_ref[...] = jnp.zeros_like(zero_ref)
        for e in range(2 * N_EXPERTS):
            fill = pltpu.make_async_copy(zero_ref, xs_hbm.at[pl.ds(lt_ref[e] * tm, tm), :], zsem)
            fill.start()
            fill.wait()

    for r in range(tb):
        for k in range(2):
            pltpu.make_async_copy(h_ref.at[pl.ds(r, 1), :], xs_hbm.at[pl.ds(dest_ref[k, r], 1), :], sem).start()
    for k in range(2):
        pltpu.make_async_copy(h_ref, xs_hbm.at[pl.ds(0, tb), :], sem).wait()


def _dispatch(h, dest, last_tile, n_tiles):
    t, d = h.shape
    tb = DISPATCH_TILE
    tm = MOE_TILE
    return pl.pallas_call(
        functools.partial(_dispatch_kernel, tb=tb, tm=tm),
        grid_spec=pltpu.PrefetchScalarGridSpec(
            num_scalar_prefetch=1, grid=(t // tb,),
            in_specs=[
                pl.BlockSpec((None, 2, tb), lambda i, lt: (i, 0, 0), memory_space=pltpu.SMEM),
                pl.BlockSpec((tb, d), lambda i, lt: (i, 0)),
            ],
            out_specs=pl.BlockSpec(memory_space=pl.ANY),
            scratch_shapes=[pltpu.VMEM((tm, d), F32), pltpu.SemaphoreType.DMA(()), pltpu.SemaphoreType.DMA(())]),
        out_shape=jax.ShapeDtypeStruct((n_tiles * tm, d), F32),
        compiler_params=_params(),
        name="l1_moe_dispatch",
    )(last_tile, dest, h)


def _moe_ffn_kernel(te_ref, nu_ref, x_ref, wg_ref, wu_ref, wd_ref, y_ref, xb_ref, acc_ref):
    i = pl.program_id(0)
    j = pl.program_id(1)
    nj = pl.num_programs(1)
    used = i < nu_ref[0]

    @pl.when(jnp.logical_and(used, j == 0))
    def _():
        xb_ref[...] = x_ref[...].astype(BF16)

    @pl.when(used)
    def _():
        xb = xb_ref[...]
        a = jnp.dot(xb, wg_ref[...], preferred_element_type=F32)
        b = jnp.dot(xb, wu_ref[...], preferred_element_type=F32)
        act = (_silu(a) * b).astype(BF16)
        part = jnp.dot(act, wd_ref[...], preferred_element_type=F32)

        @pl.when(j == 0)
        def _():
            acc_ref[...] = part

        @pl.when(jnp.logical_and(j > 0, j < nj - 1))
        def _():
            acc_ref[...] += part

        @pl.when(j == nj - 1)
        def _():
            y_ref[...] = acc_ref[...] + part

    @pl.when(jnp.logical_and(jnp.logical_not(used), j == nj - 1))
    def _():
        y_ref[...] = jnp.zeros_like(y_ref)


def _moe_ffn(xs, tile_expert, n_used, wg, wu, wd):
    d = xs.shape[1]
    tm = MOE_TILE
    n_tiles = xs.shape[0] // tm
    tf = MOE_FF_TILE
    f = wg.shape[2]
    nj = f // tf
    assert nj >= 2 and nj * tf == f

    def jeff(i, j, nu):
        return jnp.where(i < nu[0], j, nj - 1)

    return pl.pallas_call(
        _moe_ffn_kernel,
        grid_spec=pltpu.PrefetchScalarGridSpec(
            num_scalar_prefetch=2, grid=(n_tiles, nj),
            in_specs=[
                pl.BlockSpec((tm, d), lambda i, j, te, nu: (jnp.minimum(i, nu[0] - 1), 0)),
                pl.BlockSpec((None, d, tf), lambda i, j, te, nu: (te[i], 0, jeff(i, j, nu))),
                pl.BlockSpec((None, d, tf), lambda i, j, te, nu: (te[i], 0, jeff(i, j, nu))),
                pl.BlockSpec((None, tf, d), lambda i, j, te, nu: (te[i], jeff(i, j, nu), 0)),
            ],
            out_specs=pl.BlockSpec((tm, d), lambda i, j, te, nu: (i, 0)),
            scratch_shapes=[pltpu.VMEM((tm, d), BF16), pltpu.VMEM((tm, d), F32)]),
        out_shape=jax.ShapeDtypeStruct((n_tiles * tm, d), F32),
        compiler_params=_params(),
        name="l1_moe_ffn",
    )(tile_expert, n_used, xs, wg, wu, wd)


def _combine_kernel(dfirst_ref, dnext_ref, y_hbm, x_ref, rw_ref, gate_ref, g_ref, sh_ref, sc_ref, o_ref,
                    ya_ref, yb_ref, sem, *, tb):
    i = pl.program_id(0)
    n = pl.num_programs(0)
    slot = i % 2

    def row_copies(dest_ref, r, buf_slot):
        ca = pltpu.make_async_copy(y_hbm.at[pl.ds(dest_ref[0, r], 1), :], ya_ref.at[buf_slot, pl.ds(r, 1), :],
                                   sem.at[buf_slot])
        cb = pltpu.make_async_copy(y_hbm.at[pl.ds(dest_ref[1, r], 1), :], yb_ref.at[buf_slot, pl.ds(r, 1), :],
                                   sem.at[buf_slot])
        return ca, cb

    @pl.when(i == 0)
    def _():
        def issue(r, c):
            ca, cb = row_copies(dfirst_ref, r, 0)
            ca.start()
            cb.start()
            return c
        lax.fori_loop(0, tb, issue, 0)

    def wait_slot(buf_slot):
        pltpu.make_async_copy(y_hbm.at[pl.ds(0, tb), :], ya_ref.at[buf_slot], sem.at[buf_slot]).wait()
        pltpu.make_async_copy(y_hbm.at[pl.ds(0, tb), :], yb_ref.at[buf_slot], sem.at[buf_slot]).wait()

    wait_slot(slot)
    for r in range(tb):
        ca, cb = row_copies(dnext_ref, r, 1 - slot)
        ca.start()
        cb.start()
    rw = rw_ref[...]
    f = rw[:, 0:1] * ya_ref[slot] + rw[:, 1:2] * yb_ref[slot]
    x = x_ref[...] + gate_ref[...] * f
    o_ref[...] = _rms_mod(x, g_ref[...], sh_ref[...], sc_ref[...])

    @pl.when(i == n - 1)
    def _():
        wait_slot(1 - slot)


def _combine(dest, ys, x, rw, gate, g, sh, sc, seq):
    t, d = x.shape
    tb = COMBINE_TILE
    per_b = seq // tb
    n = t // tb
    vec = lambda: pl.BlockSpec((1, d), lambda i: (0, 0))
    bvec = lambda: pl.BlockSpec((None, 1, d), lambda i: (i // per_b, 0, 0))
    return pl.pallas_call(
        functools.partial(_combine_kernel, tb=tb),
        grid=(n,),
        in_specs=[
            pl.BlockSpec((None, 2, tb), lambda i: (0, 0, 0), memory_space=pltpu.SMEM),
            pl.BlockSpec((None, 2, tb), lambda i: (jnp.minimum(i + 1, n - 1), 0, 0), memory_space=pltpu.SMEM),
            pl.BlockSpec(memory_space=pl.ANY),
            pl.BlockSpec((tb, d), lambda i: (i, 0)),
            pl.BlockSpec((tb, LANES), lambda i: (i, 0)),
            bvec(), vec(), bvec(), bvec(),
        ],
        out_specs=pl.BlockSpec((tb, d), lambda i: (i, 0)),
        out_shape=jax.ShapeDtypeStruct((t, d), F32),
        scratch_shapes=[pltpu.VMEM((2, tb, d), F32), pltpu.VMEM((2, tb, d), F32), pltpu.SemaphoreType.DMA((2,))],
        compiler_params=_params(),
        name="l1_moe_combine",
    )(dest, dest, ys, x, rw, gate, g.reshape(1, d), sh, sc)


def _moe_plan(idx, t):
    tm = MOE_TILE
    n_tiles = (2 * t) // tm + N_EXPERTS
    e_flat = idx.reshape(-1)
    experts = jnp.arange(N_EXPERTS, dtype=jnp.int32)
    onehot = (e_flat[:, None] == experts[None, :]).astype(jnp.int32)
    incl = jnp.cumsum(onehot, axis=0)
    counts = incl[-1]
    pos = jnp.sum((incl - onehot) * onehot, axis=1)
    tiles_per_e = (counts + tm - 1) // tm
    tile_end = jnp.cumsum(tiles_per_e)
    row_start = (tile_end - tiles_per_e) * tm
    dest = (row_start[e_flat] + pos).astype(jnp.int32)
    n_used = tile_end[-1:]
    tile_ids = jnp.arange(n_tiles, dtype=jnp.int32)
    tile_expert = jnp.minimum(jnp.sum((tile_ids[:, None] >= tile_end[None, :]).astype(jnp.int32), axis=1),
                              N_EXPERTS - 1)
    last_e = jnp.max(jnp.where(counts > 0, experts, 0))
    tile_expert = jnp.where(tile_ids < n_used[0], tile_expert, last_e).astype(jnp.int32)
    last_tile = jnp.maximum(tile_end - 1, 0)
    unused = jnp.minimum(n_used[0] + experts, n_tiles - 1)
    clear_tiles = jnp.concatenate([last_tile, unused]).astype(jnp.int32)
    return dest, tile_expert, n_used.astype(jnp.int32), clear_tiles, n_tiles


def _by_tile(dest, t, tb):
    return dest.reshape(t // tb, tb, 2).transpose(0, 2, 1)


def kernel(x, c, l0_ada_w, l0_ada_b, l0_norm1_g, l0_norm2_g, l0_ret_w_in, l0_ret_decay_logit, l0_ret_gn_g, l0_ret_w_out, l0_ffn_w_gate, l0_ffn_w_up, l0_ffn_w_down, l1_ada_w, l1_ada_b, l1_norm1_g, l1_norm2_g, l1_attn_w_qkv, l1_attn_q_norm_g, l1_attn_k_norm_g, l1_attn_w_out, l1_moe_w_router, l1_moe_b_router, l1_moe_w_gate, l1_moe_w_up, l1_moe_w_down, final_ada_w, final_ada_b, final_norm_g):
    bsz, seq, d = x.shape
    t = bsz * seq
    xf = x.reshape(t, d)

    c8 = jnp.zeros((8, d), F32).at[:bsz].set(c)

    def mods(w, b, n):
        m = _adaln(c8, w, b)[:bsz]
        return [m[:, i * d:(i + 1) * d].reshape(bsz, 1, d) for i in range(n)]

    sh1, sc1, g1, sh2, sc2, g2 = mods(l0_ada_w, l0_ada_b, 6)
    sh3, sc3, g3, sh4, sc4, g4 = mods(l1_ada_w, l1_ada_b, 6)
    fsh, fsc = mods(final_ada_w, final_ada_b, 2)

    bf = lambda w: w.astype(BF16)

    proj = _inproj0(xf, l0_norm1_g, sh1, sc1, bf(l0_ret_w_in), seq)
    log_gamma = jax.nn.log_sigmoid(l0_ret_decay_logit.astype(F32))
    u = _retention(proj, log_gamma, l0_ret_gn_g, bsz, seq)
    x1, h2 = _outproj(u, bf(l0_ret_w_out), xf, g1, l0_norm2_g, sh2, sc2, seq)
    x2, h3 = _ffn0(h2, bf(l0_ffn_w_gate), bf(l0_ffn_w_up), bf(l0_ffn_w_down), x1, g2, l1_norm1_g, sh3, sc3, seq)

    qkv, norm2 = _qkv1(h3, bf(l1_attn_w_qkv), l1_attn_q_norm_g, l1_attn_k_norm_g, seq)
    o = _attention(qkv, norm2, bsz, seq)
    wr = jnp.zeros((d, LANES), F32).at[:, :N_EXPERTS].set(l1_moe_w_router)
    wr_hi = wr.astype(BF16)
    wr_lo = (wr - wr_hi.astype(F32)).astype(BF16)
    br = jnp.full((1, LANES), -jnp.inf, F32).at[0, :N_EXPERTS].set(l1_moe_b_router.astype(F32))
    x3, h4, idx, rw = _outproj(o, bf(l1_attn_w_out), x2, g3, l1_norm2_g, sh4, sc4, seq,
                               router=(jnp.concatenate([wr_hi, wr_lo], axis=1), br), h_dtype=F32)
    dest, tile_expert, n_used, clear_tiles, n_tiles = _moe_plan(idx[:, :2], t)
    xs = _dispatch(h4, _by_tile(dest, t, DISPATCH_TILE), clear_tiles, n_tiles)
    ys = _moe_ffn(xs, tile_expert, n_used, bf(l1_moe_w_gate), bf(l1_moe_w_up), bf(l1_moe_w_down))
    out = _combine(_by_tile(dest, t, COMBINE_TILE), ys, x3, rw, g4, final_norm_g, fsh, fsc, seq)
    return out.reshape(bsz, seq, d)
```

```python
import functools

import numpy as np
import jax
import jax.numpy as jnp
from jax import lax
from jax.experimental import pallas as pl
from jax.experimental.pallas import tpu as pltpu

GRID_W = 64
NORM_EPS = 1e-6
ROPE_THETA = 10000.0
RET_HEADS = 4
RET_DK = 256
RET_DV = 512
RET_CHUNK = 128
ATT_HEADS = 8
ATT_KV_HEADS = 2
ATT_HD = 128
ATT_GROUP = ATT_HEADS // ATT_KV_HEADS
N_EXPERTS = 8
LANES = 128
LOG2_E = 1.4426950408889634

F32 = jnp.float32
BF16 = jnp.bfloat16
HIGHEST = lax.Precision.HIGHEST

TOKEN_TILE = 512
RET_BLOCK = 512
ATT_Q_TILE_BOUNDED = 512
ATT_Q_TILE_ONLINE = 256
ATT_K_TILE = 1024
ATT_SLOTS = 4
ATT_ONES_ROWS = 16
ATT_BOUND_LIMIT = 60.0
MOE_TILE = 512
MOE_FF_TILE = 1792
DISPATCH_TILE = 512
COMBINE_TILE = 256
VMEM_LIMIT = 56 * 1024 * 1024


def _params(**kw):
    return pltpu.CompilerParams(vmem_limit_bytes=VMEM_LIMIT, **kw)


def _resident(shape):
    return pl.BlockSpec(shape, lambda *_: (0,) * len(shape), pipeline_mode=pl.Buffered(1))


def _silu(a):
    return a * jax.nn.sigmoid(a)


def _rms_mod(x, g, shift, scale):
    y = x * lax.rsqrt(jnp.mean(x * x, axis=-1, keepdims=True) + NORM_EPS)
    return (y * g) * (1.0 + scale) + shift


def _adaln_kernel(c_ref, w_ref, b_ref, o_ref):
    a = _silu(c_ref[...])
    o_ref[...] = jnp.dot(a, w_ref[...], precision=HIGHEST, preferred_element_type=F32) + b_ref[...]


def _adaln(c8, w, b):
    d, n = w.shape
    tn = 1024
    return pl.pallas_call(
        _adaln_kernel,
        grid=(n // tn,),
        in_specs=[
            pl.BlockSpec((8, d), lambda j: (0, 0)),
            pl.BlockSpec((d, tn), lambda j: (0, j)),
            pl.BlockSpec((1, tn), lambda j: (0, j)),
        ],
        out_specs=pl.BlockSpec((8, tn), lambda j: (0, j)),
        out_shape=jax.ShapeDtypeStruct((8, n), F32),
        compiler_params=_params(),
        name="adaln",
    )(c8, w, b.reshape(1, n))


def _rope_tables(seq, half):
    nfreq = half // 2
    freqs = ROPE_THETA ** (-np.arange(0, nfreq, dtype=np.float64) * 2.0 / half)
    npos = max(seq // GRID_W, GRID_W)
    ang = np.arange(npos, dtype=np.float64)[:, None] * freqs[None, :]
    cos = np.concatenate([np.cos(ang), np.cos(ang)], axis=1)
    sin = np.concatenate([-np.sin(ang), np.sin(ang)], axis=1)
    return cos.astype(np.float32), sin.astype(np.float32)


def _inproj0_kernel(x_ref, g_ref, sh_ref, sc_ref, w_ref, cr_ref, sr_ref, cc_ref, scol_ref, o_ref, *, tm):
    h = _rms_mod(x_ref[...], g_ref[...], sh_ref[...], sc_ref[...]).astype(BF16)
    rows = tm // GRID_W
    cr = cr_ref[...]
    sr = sr_ref[...]
    cc = cc_ref[...][None]
    scol = scol_ref[...][None]
    nw = 512
    n_qk = 2 * RET_HEADS * RET_DK
    for c in range(o_ref.shape[1] // nw):
        y = jnp.dot(h, w_ref[:, c * nw:(c + 1) * nw], preferred_element_type=F32)
        if c * nw < n_qk:
            is_k = c * nw >= RET_HEADS * RET_DK
            for s in range(nw // LANES):
                ys = y[:, s * LANES:(s + 1) * LANES]
                y3 = ys.reshape(rows, GRID_W, LANES)
                r3 = pltpu.roll(ys, LANES // 2, 1).reshape(rows, GRID_W, LANES)
                if s % 2 == 0:
                    o3 = y3 * cr + r3 * sr
                else:
                    o3 = y3 * cc + r3 * scol
                if is_k:
                    o3 = o3 * (RET_DK ** -0.5)
                o_ref[:, c * nw + s * LANES:c * nw + (s + 1) * LANES] = o3.reshape(tm, LANES).astype(BF16)
        else:
            o_ref[:, c * nw:(c + 1) * nw] = y.astype(BF16)


def _inproj0(x, g, sh, sc, w_bf16, seq):
    t, d = x.shape
    n = w_bf16.shape[1]
    tm = TOKEN_TILE
    per_b = seq // tm
    rows = tm // GRID_W
    cos, sin = _rope_tables(seq, LANES)
    n_img_rows = seq // GRID_W
    cr = jnp.asarray(cos[:n_img_rows].reshape(n_img_rows, 1, LANES))
    sr = jnp.asarray(sin[:n_img_rows].reshape(n_img_rows, 1, LANES))
    cc = jnp.asarray(cos[:GRID_W])
    scol = jnp.asarray(sin[:GRID_W])
    vec = lambda: pl.BlockSpec((1, d), lambda i: (0, 0))
    bvec = lambda: pl.BlockSpec((None, 1, d), lambda i: (i // per_b, 0, 0))
    rowt = lambda: pl.BlockSpec((rows, 1, LANES), lambda i: (i % per_b, 0, 0))
    colt = lambda: pl.BlockSpec((GRID_W, LANES), lambda i: (0, 0))
    return pl.pallas_call(
        functools.partial(_inproj0_kernel, tm=tm),
        grid=(t // tm,),
        in_specs=[pl.BlockSpec((tm, d), lambda i: (i, 0)), vec(), bvec(), bvec(),
                  _resident((d, n)), rowt(), rowt(), colt(), colt()],
        out_specs=pl.BlockSpec((tm, n), lambda i: (i, 0)),
        out_shape=jax.ShapeDtypeStruct((t, n), BF16),
        compiler_params=_params(),
        name="l0_inproj",
    )(x, g.reshape(1, d), sh, sc, w_bf16, cr, sr, cc, scol)


def _col(n):
    return lax.broadcasted_iota(jnp.int32, (n, 1), 0).astype(F32)


def _ret_fwd_kernel(lg_ref, q_ref, k_ref, v_ref, y_ref, state_ref, *, nchunks):
    @pl.when(pl.program_id(1) == 0)
    def _():
        state_ref[...] = jnp.zeros_like(state_ref)

    c_len = RET_CHUNK
    ii = lax.broadcasted_iota(jnp.int32, (c_len, c_len), 0)
    jj = lax.broadcasted_iota(jnp.int32, (c_len, c_len), 1)
    diff = (ii - jj).astype(F32)
    idx = _col(c_len)
    consts = []
    for hd in range(RET_HEADS):
        lgf = lg_ref[0, hd]
        lgb = lg_ref[1, hd]
        dmat = jnp.where(diff >= 0, jnp.exp(diff * lgf), jnp.exp(-diff * lgb))
        xi = jnp.exp((idx + 1.0) * lgf)
        zeta = jnp.exp((c_len - 1.0 - idx) * lgf)
        decay = jnp.exp(jnp.full((1, 1), c_len, F32) * lgf)
        consts.append((dmat, xi, zeta, decay))
    for c in range(nchunks):
        sl = slice(c * c_len, (c + 1) * c_len)
        for hd in range(RET_HEADS):
            dmat, xi, zeta, decay = consts[hd]
            qk = slice(hd * RET_DK, (hd + 1) * RET_DK)
            vv = slice(hd * RET_DV, (hd + 1) * RET_DV)
            q = q_ref[sl, qk]
            k = k_ref[sl, qk]
            v = v_ref[sl, vv]
            s = lax.dot_general(q, k, (((1,), (1,)), ((), ())), preferred_element_type=F32) * dmat
            y = jnp.dot(s.astype(BF16), v, preferred_element_type=F32)
            st = state_ref[hd]
            y = y + jnp.dot(q, st.astype(BF16), preferred_element_type=F32) * xi
            kz = (k.astype(F32) * zeta).astype(BF16)
            upd = lax.dot_general(kz, v, (((0,), (0,)), ((), ())), preferred_element_type=F32)
            state_ref[hd] = st * decay + upd
            y_ref[sl, vv] = y.astype(y_ref.dtype)


def _ret_bwd_kernel(lg_ref, q_ref, k_ref, v_ref, g_ref, yf_ref, gn_ref, u_ref, state_ref, *, nchunks):
    @pl.when(pl.program_id(1) == 0)
    def _():
        state_ref[...] = jnp.zeros_like(state_ref)

    c_len = RET_CHUNK
    idx = _col(c_len)
    consts = []
    for hd in range(RET_HEADS):
        lgb = lg_ref[1, hd]
        consts.append((jnp.exp((c_len - idx) * lgb), jnp.exp(idx * lgb),
                       jnp.exp(jnp.full((1, 1), c_len, F32) * lgb)))
    for c in reversed(range(nchunks)):
        sl = slice(c * c_len, (c + 1) * c_len)
        for hd in range(RET_HEADS):
            xi, zeta, decay = consts[hd]
            qk = slice(hd * RET_DK, (hd + 1) * RET_DK)
            vv = slice(hd * RET_DV, (hd + 1) * RET_DV)
            q = q_ref[sl, qk]
            k = k_ref[sl, qk]
            v = v_ref[sl, vv]
            st = state_ref[hd]
            y = yf_ref[sl, vv].astype(F32) + jnp.dot(q, st.astype(BF16), preferred_element_type=F32) * xi
            kz = (k.astype(F32) * zeta).astype(BF16)
            upd = lax.dot_general(kz, v, (((0,), (0,)), ((), ())), preferred_element_type=F32)
            state_ref[hd] = st * decay + upd
            mu = jnp.mean(y, axis=-1, keepdims=True)
            yc = y - mu
            var = jnp.mean(yc * yc, axis=-1, keepdims=True)
            yn = yc * lax.rsqrt(var + NORM_EPS) * gn_ref[:, vv]
            u_ref[sl, vv] = (_silu(g_ref[sl, vv].astype(F32)) * yn).astype(u_ref.dtype)


def _retention(proj, log_gamma, gn_g, bsz, seq):
    t = proj.shape[0]
    tb = RET_BLOCK
    nb = seq // tb
    nchunks = tb // RET_CHUNK
    n_qk = RET_HEADS * RET_DK
    n_v = RET_HEADS * RET_DV
    assert n_v == 2 * n_qk
    grid = (bsz, nb)
    qspec = lambda f: pl.BlockSpec((tb, n_qk), lambda b, n, lg: (b * nb + f(n), 0))
    kspec = lambda f: pl.BlockSpec((tb, n_qk), lambda b, n, lg: (b * nb + f(n), 1))
    vspec = lambda f: pl.BlockSpec((tb, n_v), lambda b, n, lg: (b * nb + f(n), 1))
    gspec = lambda f: pl.BlockSpec((tb, n_v), lambda b, n, lg: (b * nb + f(n), 2))
    yspec = lambda f: pl.BlockSpec((tb, n_v), lambda b, n, lg: (b * nb + f(n), 0))
    fwd = lambda n: n
    rev = lambda n: nb - 1 - n
    state = pltpu.VMEM((RET_HEADS, RET_DK, RET_DV), F32)
    yf = pl.pallas_call(
        functools.partial(_ret_fwd_kernel, nchunks=nchunks),
        grid_spec=pltpu.PrefetchScalarGridSpec(
            num_scalar_prefetch=1, grid=grid,
            in_specs=[qspec(fwd), kspec(fwd), vspec(fwd)],
            out_specs=yspec(fwd), scratch_shapes=[state]),
        out_shape=jax.ShapeDtypeStruct((t, RET_HEADS * RET_DV), BF16),
        compiler_params=_params(),
        name="l0_ret_fwd",
    )(log_gamma, proj, proj, proj)
    return pl.pallas_call(
        functools.partial(_ret_bwd_kernel, nchunks=nchunks),
        grid_spec=pltpu.PrefetchScalarGridSpec(
            num_scalar_prefetch=1, grid=grid,
            in_specs=[qspec(rev), kspec(rev), vspec(rev), gspec(rev), yspec(rev),
                      pl.BlockSpec((1, n_v), lambda b, n, lg: (0, 0))],
            out_specs=yspec(rev), scratch_shapes=[state]),
        out_shape=jax.ShapeDtypeStruct((t, RET_HEADS * RET_DV), BF16),
        compiler_params=_params(),
        name="l0_ret_bwd",
    )(log_gamma, proj, proj, proj, proj, yf, gn_g.reshape(1, -1))


def _top2(logits):
    lane = lax.broadcasted_iota(jnp.int32, logits.shape, 1)
    v1 = jnp.max(logits, axis=-1, keepdims=True)
    i1 = jnp.min(jnp.where(logits == v1, lane, LANES), axis=-1, keepdims=True)
    rest = jnp.where(lane == i1, -jnp.inf, logits)
    v2 = jnp.max(rest, axis=-1, keepdims=True)
    i2 = jnp.min(jnp.where(rest == v2, lane, LANES), axis=-1, keepdims=True)
    e = jnp.exp(v2 - v1)
    w1 = 1.0 / (1.0 + e)
    w2 = e / (1.0 + e)
    idx = jnp.where(lane == 0, i1, jnp.where(lane == 1, i2, 0))
    w = jnp.where(lane == 0, w1, jnp.where(lane == 1, w2, 0.0))
    return idx, w


def _outproj_kernel(u_ref, w_ref, x_ref, gate_ref, g_ref, sh_ref, sc_ref, *rest, router):
    if router:
        wr_ref, br_ref, xo_ref, ho_ref, idx_ref, rw_ref = rest
    else:
        xo_ref, ho_ref = rest
    m = jnp.dot(u_ref[...], w_ref[...], preferred_element_type=F32)
    x1 = x_ref[...] + gate_ref[...] * m
    xo_ref[...] = x1
    h = _rms_mod(x1, g_ref[...], sh_ref[...], sc_ref[...])
    ho_ref[...] = h.astype(ho_ref.dtype)
    if router:
        h_hi = h.astype(BF16)
        h_lo = (h - h_hi.astype(F32)).astype(BF16)
        a = jnp.dot(h_hi, wr_ref[...], preferred_element_type=F32)
        b = jnp.dot(h_lo, wr_ref[:, :LANES], preferred_element_type=F32)
        logits = (a[:, :LANES] + a[:, LANES:]) + b + br_ref[...]
        idx, w = _top2(logits)
        idx_ref[...] = idx
        rw_ref[...] = w


def _outproj(u, w_bf16, x, gate, g, sh, sc, seq, router=None, h_dtype=BF16):
    t, d = x.shape
    kdim = u.shape[1]
    tm = TOKEN_TILE
    per_b = seq // tm
    row = lambda width: pl.BlockSpec((tm, width), lambda i: (i, 0))
    vec = lambda: pl.BlockSpec((1, d), lambda i: (0, 0))
    bvec = lambda: pl.BlockSpec((None, 1, d), lambda i: (i // per_b, 0, 0))
    in_specs = [row(kdim), _resident((kdim, d)), row(d), bvec(), vec(), bvec(), bvec()]
    args = [u, w_bf16, x, gate, g.reshape(1, d), sh, sc]
    out_specs = [row(d), row(d)]
    out_shape = [jax.ShapeDtypeStruct((t, d), F32), jax.ShapeDtypeStruct((t, d), h_dtype)]
    if router is not None:
        wr, br = router
        in_specs += [_resident((d, 2 * LANES)), pl.BlockSpec((1, LANES), lambda i: (0, 0))]
        args += [wr, br]
        out_specs += [row(LANES), row(LANES)]
        out_shape += [jax.ShapeDtypeStruct((t, LANES), jnp.int32), jax.ShapeDtypeStruct((t, LANES), F32)]
    return pl.pallas_call(
        functools.partial(_outproj_kernel, router=router is not None),
        grid=(t // tm,),
        in_specs=in_specs, out_specs=out_specs, out_shape=out_shape,
        compiler_params=_params(),
        name="outproj_router" if router is not None else "outproj",
    )(*args)


def _ffn0_kernel(h_ref, wg_ref, wu_ref, wd_ref, x_ref, gate_ref, g_ref, sh_ref, sc_ref, xo_ref, ho_ref, *, tf):
    h = h_ref[...]
    acc = jnp.zeros(x_ref.shape, F32)
    for c in range(wg_ref.shape[1] // tf):
        cs = slice(c * tf, (c + 1) * tf)
        a = jnp.dot(h, wg_ref[:, cs], preferred_element_type=F32)
        b = jnp.dot(h, wu_ref[:, cs], preferred_element_type=F32)
        act = (_silu(a) * b).astype(BF16)
        acc = acc + jnp.dot(act, wd_ref[cs, :], preferred_element_type=F32)
    x2 = x_ref[...] + gate_ref[...] * acc
    xo_ref[...] = x2
    ho_ref[...] = _rms_mod(x2, g_ref[...], sh_ref[...], sc_ref[...]).astype(ho_ref.dtype)


def _ffn0(h, wg, wu, wd, x, gate, g, sh, sc, seq):
    t, d = x.shape
    f = wg.shape[1]
    tm = TOKEN_TILE
    per_b = seq // tm
    row = lambda: pl.BlockSpec((tm, d), lambda i: (i, 0))
    vec = lambda: pl.BlockSpec((1, d), lambda i: (0, 0))
    bvec = lambda: pl.BlockSpec((None, 1, d), lambda i: (i // per_b, 0, 0))
    return pl.pallas_call(
        functools.partial(_ffn0_kernel, tf=f),
        grid=(t // tm,),
        in_specs=[row(), _resident((d, f)), _resident((d, f)), _resident((f, d)), row(), bvec(), vec(), bvec(), bvec()],
        out_specs=[row(), row()],
        out_shape=[jax.ShapeDtypeStruct((t, d), F32), jax.ShapeDtypeStruct((t, d), BF16)],
        compiler_params=_params(),
        name="l0_ffn",
    )(h, wg, wu, wd, x, gate, g.reshape(1, d), sh, sc)


def _qkv1_kernel(h_ref, w_ref, qg_ref, kg_ref, cr_ref, s1r_ref, s2r_ref, cc_ref, s1c_ref, s2c_ref, o_ref, n2_ref,
                 *, tm):
    rows = tm // GRID_W
    y = jnp.dot(h_ref[...], w_ref[...], preferred_element_type=F32)
    cos =(cr_ref[...] + cc_ref[...][None]).reshape(tm, LANES)
    s1 = (s1r_ref[...] + s1c_ref[...][None]).reshape(tm, LANES)
    s2 = (s2r_ref[...] + s2c_ref[...][None]).reshape(tm, LANES)
    n_rot = ATT_HEADS + ATT_KV_HEADS
    lane = lax.broadcasted_iota(jnp.int32, (tm, LANES), 1)
    norm2 = jnp.zeros((tm, LANES), F32)
    ones = jnp.ones((LANES, LANES), BF16)

    def row_sums(a):
        a_hi = a.astype(BF16)
        a_lo = (a - a_hi.astype(F32)).astype(BF16)
        return (jnp.dot(a_hi, ones, preferred_element_type=F32) + jnp.dot(a_lo, ones, preferred_element_type=F32))

    for j in range(n_rot):
        ys = y[:, j * LANES:(j + 1) * LANES]
        gain = qg_ref[...] if j < ATT_HEADS else kg_ref[...]
        yn = ys * lax.rsqrt(row_sums(ys * ys) * (1.0 / LANES) + NORM_EPS) * gain
        o = yn * cos + pltpu.roll(yn, 96, 1) * s1 + pltpu.roll(yn, 32, 1) * s2
        if j < ATT_HEADS:
            o = o * (ATT_HD ** -0.5 * LOG2_E)
        ob = o.astype(BF16)
        o_ref[:, j * LANES:(j + 1) * LANES] = ob
        norm2 = jnp.where(lane == j, row_sums(o * o), norm2)
    o_ref[:, n_rot * LANES:] = y[:, n_rot * LANES:].astype(BF16)
    n2_ref[...] = norm2


def _att_rope_tables(seq):
    half = ATT_HD // 2
    nfreq = half // 2
    freqs = ROPE_THETA ** (-np.arange(0, nfreq, dtype=np.float64) * 2.0 / half)
    lane = np.arange(LANES)
    f_l = freqs[lane % nfreq]
    first = (lane % half) < nfreq
    is_row = lane < half
    n_img_rows = seq // GRID_W

    def tables(npos, mask):
        ang = np.arange(npos, dtype=np.float64)[:, None] * f_l[None, :]
        c = np.where(mask[None], np.cos(ang), 0.0)
        s1 = np.where((mask & first)[None], -np.sin(ang), 0.0)
        s2 = np.where((mask & ~first)[None], np.sin(ang), 0.0)
        return [a.astype(np.float32) for a in (c, s1, s2)]

    row_t = [jnp.asarray(a.reshape(n_img_rows, 1, LANES)) for a in tables(n_img_rows, is_row)]
    col_t = [jnp.asarray(a) for a in tables(GRID_W, ~is_row)]
    return row_t, col_t


def _qkv1(h, w_bf16, qg, kg, seq):
    t, d = h.shape
    n = w_bf16.shape[1]
    tm = TOKEN_TILE
    per_b = seq // tm
    rows = tm // GRID_W
    row_t, col_t = _att_rope_tables(seq)
    rowt = lambda: pl.BlockSpec((rows, 1, LANES), lambda i: (i % per_b, 0, 0))
    colt = lambda: pl.BlockSpec((GRID_W, LANES), lambda i: (0, 0))
    hvec = lambda: pl.BlockSpec((1, LANES), lambda i: (0, 0))
    return pl.pallas_call(
        functools.partial(_qkv1_kernel, tm=tm),
        grid=(t // tm,),
        in_specs=[pl.BlockSpec((tm, d), lambda i: (i, 0)), _resident((d, n)), hvec(), hvec(),
                  rowt(), rowt(), rowt(), colt(), colt(), colt()],
        out_specs=[pl.BlockSpec((tm, n), lambda i: (i, 0)), pl.BlockSpec((tm, LANES), lambda i: (i, 0))],
        out_shape=[jax.ShapeDtypeStruct((t, n), BF16), jax.ShapeDtypeStruct((t, LANES), F32)],
        compiler_params=_params(),
        name="l1_qkv",
    )(h, w_bf16, qg.reshape(1, LANES), kg.reshape(1, LANES), *row_t, *col_t)


def _stack_heads(q_ref, qs_ref, tq):
    for g in range(ATT_GROUP):
        qs_ref[g * tq:(g + 1) * tq, :] = q_ref[:, g * LANES:(g + 1) * LANES]


def _pipelined(nk, scores, update):
    scores(0, 0)
    scores(1, 1)

    def body(jj, carry):
        j = ATT_SLOTS * jj
        for u in range(ATT_SLOTS):
            scores(j + u + 2, (u + 2) % ATT_SLOTS)
            update(j + u, u)
        return carry

    lax.fori_loop(0, nk // ATT_SLOTS - 1, body, 0)
    j = nk - ATT_SLOTS
    for u in range(ATT_SLOTS):
        if u + 2 < ATT_SLOTS:
            scores(j + u + 2, (u + 2) % ATT_SLOTS)
        update(j + u, u)


def _store_heads(o, o_ref, tq):
    for g in range(ATT_GROUP):
        o_ref[:, g * LANES:(g + 1) * LANES] = o[:, g * tq:(g + 1) * tq].T.astype(o_ref.dtype)


def _flash_bounded_kernel(q_ref, k_ref, vt_ref, mb_ref, o_ref, qs_ref, acc_ref, p_ref, vte_ref, *, tq, tk):
    nk = vt_ref.shape[0]

    @pl.when(pl.program_id(2) == 0)
    def _():
        for j in range(nk):
            vte_ref[j, :ATT_HD, :] = vt_ref[j]
            vte_ref[j, ATT_HD:, :] = jnp.ones((ATT_ONES_ROWS, tk), BF16)

    _stack_heads(q_ref, qs_ref, tq)
    acc_ref[...] = jnp.zeros_like(acc_ref)
    mb = mb_ref[...]

    def scores(j, slot):
        k_c = k_ref[pl.ds(pl.multiple_of(j * tk, tk), tk), :]
        s = lax.dot_general(k_c, qs_ref[...], (((1,), (1,)), ((), ())), preferred_element_type=F32)
        p_ref[slot] = jnp.exp2(s - mb).astype(BF16)

    def update(j, slot):
        acc_ref[...] += jnp.dot(vte_ref[j], p_ref[slot], preferred_element_type=F32)

    _pipelined(nk, scores, update)
    _store_heads(acc_ref[:ATT_HD, :] / acc_ref[ATT_HD:ATT_HD + 1, :], o_ref, tq)


def _flash_online_kernel(q_ref, k_ref, vt_ref, o_ref, qs_ref, m_ref, l_ref, acc_ref, s_ref, cm_ref, *, tq, tk):
    _stack_heads(q_ref, qs_ref, tq)
    m_ref[...] = jnp.full(m_ref.shape, -jnp.inf, F32)
    l_ref[...] = jnp.zeros_like(l_ref)
    acc_ref[...] = jnp.zeros_like(acc_ref)
    nk = vt_ref.shape[0]

    def scores(j, slot):
        k_c = k_ref[pl.ds(pl.multiple_of(j * tk, tk), tk), :]
        s = lax.dot_general(k_c, qs_ref[...], (((1,), (1,)), ((), ())), preferred_element_type=F32)
        s_ref[slot] = s
        cm_ref[slot] = jnp.max(s, axis=0, keepdims=True)

    def update(j, slot):
        m_old = m_ref[...]
        m_new = jnp.maximum(m_old, cm_ref[slot])
        p = jnp.exp2(s_ref[slot] - m_new)
        alpha = jnp.exp2(m_old - m_new)
        l_ref[...] = alpha * l_ref[...] + jnp.sum(p, axis=0, keepdims=True)
        acc_ref[...] = alpha * acc_ref[...] + jnp.dot(vt_ref[j], p.astype(BF16), preferred_element_type=F32)
        m_ref[...] = m_new

    _pipelined(nk, scores, update)
    _store_heads(acc_ref[...] / l_ref[...], o_ref, tq)


def _attention(qkv, norm2, bsz, seq):
    t = qkv.shape[0]
    tk = min(ATT_K_TILE, seq // ATT_SLOTS)
    nk = seq // tk
    assert nk % ATT_SLOTS == 0 and nk * tk == seq, "the score pipeline rotates through ATT_SLOTS key chunks"
    gw = ATT_GROUP * ATT_HD
    v = qkv[:, (ATT_HEADS + ATT_KV_HEADS) * ATT_HD:]
    vt = v.reshape(bsz, nk, tk, ATT_KV_HEADS, ATT_HD).transpose(0, 3, 1, 4, 2)

    tqb = min(ATT_Q_TILE_BOUNDED, seq)
    nqb = seq // tqb
    qn = jnp.sqrt(norm2[:, :ATT_HEADS]).reshape(bsz, nqb, tqb, ATT_KV_HEADS, ATT_GROUP)
    kn = jnp.sqrt(jnp.max(norm2[:, ATT_HEADS:ATT_HEADS + ATT_KV_HEADS].reshape(bsz, seq, ATT_KV_HEADS), axis=1))
    mb = qn.transpose(0, 3, 1, 4, 2) * kn[:, :, None, None, None]
    mb = mb.reshape(bsz * ATT_KV_HEADS * nqb, 1, ATT_GROUP * tqb)

    def call(body, tq, scratch, bound=None):
        nq = seq // tq
        nqs = ATT_GROUP * tq
        in_specs = [
            pl.BlockSpec((tq, gw), lambda b, kv, i: (b * nq + i, kv)),
            pl.BlockSpec((seq, ATT_HD), lambda b, kv, i: (b, ATT_HEADS + kv)),
            pl.BlockSpec((None, None, nk, ATT_HD, tk), lambda b, kv, i: (b, kv, 0, 0, 0)),
        ]
        args = [qkv, qkv, vt]
        if bound is not None:
            in_specs.append(pl.BlockSpec((None, 1, nqs), lambda b, kv, i: ((b * ATT_KV_HEADS + kv) * nq + i, 0, 0)))
            args.append(bound)
        return pl.pallas_call(
            functools.partial(body, tq=tq, tk=tk),
            grid=(bsz, ATT_KV_HEADS, nq),
            in_specs=in_specs,
            out_specs=pl.BlockSpec((tq, gw), lambda b, kv, i: (b * nq + i, kv)),
            out_shape=jax.ShapeDtypeStruct((t, ATT_HEADS * ATT_HD), BF16),
            scratch_shapes=[pltpu.VMEM((nqs, ATT_HD), BF16)] + scratch(nqs),
            compiler_params=_params(),
            name="l1_attention",
        )(*args)

    def bounded():
        return call(_flash_bounded_kernel, tqb,
                    lambda nqs: [pltpu.VMEM((ATT_HD + ATT_ONES_ROWS, nqs), F32), pltpu.VMEM((ATT_SLOTS, tk, nqs), BF16),
                                 pltpu.VMEM((nk, ATT_HD + ATT_ONES_ROWS, tk), BF16)], bound=mb)

    def online():
        return call(_flash_online_kernel, min(ATT_Q_TILE_ONLINE, seq),
                    lambda nqs: [pltpu.VMEM((1, nqs), F32), pltpu.VMEM((1, nqs), F32), pltpu.VMEM((ATT_HD, nqs), F32),
                                 pltpu.VMEM((ATT_SLOTS, tk, nqs), F32), pltpu.VMEM((ATT_SLOTS, 1, nqs), F32)])

    return lax.cond(jnp.max(mb) <= ATT_BOUND_LIMIT, bounded, online)


def _dispatch_kernel(lt_ref, dest_ref, h_ref, xs_hbm, zero_ref, zsem, sem, *, tb, tm):
    @pl.when(pl.program_id(0) == 0)
    def _():
        zero_ref[...] = jnp.zeros_like(zero_ref)
        for e in range(2 * N_EXPERTS):
            fill = pltpu.make_async_copy(zero_ref, xs_hbm.at[pl.ds(lt_ref[e] * tm, tm), :], zsem)
            fill.start()
            fill.wait()

    for r in range(tb):
        for k in range(2):
            pltpu.make_async_copy(h_ref.at[pl.ds(r, 1), :], xs_hbm.at[pl.ds(dest_ref[k, r], 1), :], sem).start()
    for k in range(2):
        pltpu.make_async_copy(h_ref, xs_hbm.at[pl.ds(0, tb), :], sem).wait()


def _dispatch(h, dest, last_tile, n_tiles):
    t, d = h.shape
    tb = DISPATCH_TILE
    tm = MOE_TILE
    return pl.pallas_call(
        functools.partial(_dispatch_kernel, tb=tb, tm=tm),
        grid_spec=pltpu.PrefetchScalarGridSpec(
            num_scalar_prefetch=1, grid=(t // tb,),
            in_specs=[
                pl.BlockSpec((None, 2, tb), lambda i, lt: (i, 0, 0), memory_space=pltpu.SMEM),
                pl.BlockSpec((tb, d), lambda i, lt: (i, 0)),
            ],
            out_specs=pl.BlockSpec(memory_space=pl.ANY),
            scratch_shapes=[pltpu.VMEM((tm, d), F32), pltpu.SemaphoreType.DMA(()), pltpu.SemaphoreType.DMA(())]),
        out_shape=jax.ShapeDtypeStruct((n_tiles * tm, d), F32),
        compiler_params=_params(),
        name="l1_moe_dispatch",
    )(last_tile, dest, h)


def _moe_ffn_kernel(te_ref, nu_ref, x_ref, wg_ref, wu_ref, wd_ref, y_ref, xb_ref, acc_ref):
    i = pl.program_id(0)
    j = pl.program_id(1)
    nj = pl.num_programs(1)
    used = i < nu_ref[0]

    @pl.when(jnp.logical_and(used, j == 0))
    def _():
        xb_ref[...] = x_ref[...].astype(BF16)

    @pl.when(used)
    def _():
        xb = xb_ref[...]
        a = jnp.dot(xb, wg_ref[...], preferred_element_type=F32)
        b = jnp.dot(xb, wu_ref[...], preferred_element_type=F32)
        act = (_silu(a) * b).astype(BF16)
        part = jnp.dot(act, wd_ref[...], preferred_element_type=F32)

        @pl.when(j == 0)
        def _():
            acc_ref[...] = part

        @pl.when(jnp.logical_and(j > 0, j < nj - 1))
        def _():
            acc_ref[...] += part

        @pl.when(j == nj - 1)
        def _():
            y_ref[...] = acc_ref[...] + part

    @pl.when(jnp.logical_and(jnp.logical_not(used), j == nj - 1))
    def _():
        y_ref[...] = jnp.zeros_like(y_ref)


def _moe_ffn(xs, tile_expert, n_used, wg, wu, wd):
    d = xs.shape[1]
    tm = MOE_TILE
    n_tiles = xs.shape[0] // tm
    tf = MOE_FF_TILE
    f = wg.shape[2]
    nj = f // tf
    assert nj >= 2 and nj * tf == f

    def jeff(i, j, nu):
        return jnp.where(i < nu[0], j, nj - 1)

    return pl.pallas_call(
        _moe_ffn_kernel,
        grid_spec=pltpu.PrefetchScalarGridSpec(
            num_scalar_prefetch=2, grid=(n_tiles, nj),
            in_specs=[
                pl.BlockSpec((tm, d), lambda i, j, te, nu: (jnp.minimum(i, nu[0] - 1), 0)),
                pl.BlockSpec((None, d, tf), lambda i, j, te, nu: (te[i], 0, jeff(i, j, nu))),
                pl.BlockSpec((None, d, tf), lambda i, j, te, nu: (te[i], 0, jeff(i, j, nu))),
                pl.BlockSpec((None, tf, d), lambda i, j, te, nu: (te[i], jeff(i, j, nu), 0)),
            ],
            out_specs=pl.BlockSpec((tm, d), lambda i, j, te, nu: (i, 0)),
            scratch_shapes=[pltpu.VMEM((tm, d), BF16), pltpu.VMEM((tm, d), F32)]),
        out_shape=jax.ShapeDtypeStruct((n_tiles * tm, d), F32),
        compiler_params=_params(),
        name="l1_moe_ffn",
    )(tile_expert, n_used, xs, wg, wu, wd)


def _combine_kernel(dfirst_ref, dnext_ref, y_hbm, x_ref, rw_ref, gate_ref, g_ref, sh_ref, sc_ref, o_ref,
                    ya_ref, yb_ref, sem, *, tb):
    i = pl.program_id(0)
    n = pl.num_programs(0)
    slot = i % 2

    def row_copies(dest_ref, r, buf_slot):
        ca = pltpu.make_async_copy(y_hbm.at[pl.ds(dest_ref[0, r], 1), :], ya_ref.at[buf_slot, pl.ds(r, 1), :],
                                   sem.at[buf_slot])
        cb = pltpu.make_async_copy(y_hbm.at[pl.ds(dest_ref[1, r], 1), :], yb_ref.at[buf_slot, pl.ds(r, 1), :],
                                   sem.at[buf_slot])
        return ca, cb

    @pl.when(i == 0)
    def _():
        def issue(r, c):
            ca, cb = row_copies(dfirst_ref, r, 0)
            ca.start()
            cb.start()
            return c
        lax.fori_loop(0, tb, issue, 0)

    @pl.when(i + 1 < n)
    def _():
        for r in range(tb):
            ca, cb = row_copies(dnext_ref, r, 1 - slot)
            ca.start()
            cb.start()

    pltpu.make_async_copy(y_hbm.at[pl.ds(0, tb), :], ya_ref.at[slot], sem.at[slot]).wait()
    pltpu.make_async_copy(y_hbm.at[pl.ds(0, tb), :], yb_ref.at[slot], sem.at[slot]).wait()
    rw = rw_ref[...]
    f = rw[:, 0:1] * ya_ref[slot] + rw[:, 1:2] * yb_ref[slot]
    x = x_ref[...] + gate_ref[...] * f
    o_ref[...] = _rms_mod(x, g_ref[...], sh_ref[...], sc_ref[...])


def _combine(dest, ys, x, rw, gate, g, sh, sc, seq):
    t, d = x.shape
    tb = COMBINE_TILE
    per_b = seq // tb
    n = t // tb
    vec = lambda: pl.BlockSpec((1, d), lambda i: (0, 0))
    bvec = lambda: pl.BlockSpec((None, 1, d), lambda i: (i // per_b, 0, 0))
    return pl.pallas_call(
        functools.partial(_combine_kernel, tb=tb),
        grid=(n,),
        in_specs=[
            pl.BlockSpec((None, 2, tb), lambda i: (0, 0, 0), memory_space=pltpu.SMEM),
            pl.BlockSpec((None, 2, tb), lambda i: (jnp.minimum(i + 1, n - 1), 0, 0), memory_space=pltpu.SMEM),
            pl.BlockSpec(memory_space=pl.ANY),
            pl.BlockSpec((tb, d), lambda i: (i, 0)),
            pl.BlockSpec((tb, LANES), lambda i: (i, 0)),
            bvec(), vec(), bvec(), bvec(),
        ],
        out_specs=pl.BlockSpec((tb, d), lambda i: (i, 0)),
        out_shape=jax.ShapeDtypeStruct((t, d), F32),
        scratch_shapes=[pltpu.VMEM((2, tb, d), F32), pltpu.VMEM((2, tb, d), F32), pltpu.SemaphoreType.DMA((2,))],
        compiler_params=_params(),
        name="l1_moe_combine",
    )(dest, dest, ys, x, rw, gate, g.reshape(1, d), sh, sc)


def _moe_plan(idx, t):
    tm = MOE_TILE
    n_tiles = (2 * t) // tm + N_EXPERTS
    e_flat = idx.reshape(-1)
    experts = jnp.arange(N_EXPERTS, dtype=jnp.int32)
    onehot = (e_flat[:, None] == experts[None, :]).astype(jnp.int32)
    incl = jnp.cumsum(onehot, axis=0)
    counts = incl[-1]
    pos = jnp.sum((incl - onehot) * onehot, axis=1)
    tiles_per_e = (counts + tm - 1) // tm
    tile_end = jnp.cumsum(tiles_per_e)
    row_start = (tile_end - tiles_per_e) * tm
    dest = (row_start[e_flat] + pos).astype(jnp.int32)
    n_used = tile_end[-1:]
    tile_ids = jnp.arange(n_tiles, dtype=jnp.int32)
    tile_expert = jnp.minimum(jnp.sum((tile_ids[:, None] >= tile_end[None, :]).astype(jnp.int32), axis=1),
                              N_EXPERTS - 1)
    last_e = jnp.max(jnp.where(counts > 0, experts, 0))
    tile_expert = jnp.where(tile_ids < n_used[0], tile_expert, last_e).astype(jnp.int32)
    last_tile = jnp.maximum(tile_end - 1, 0)
    unused = jnp.minimum(n_used[0] + experts, n_tiles - 1)
    clear_tiles = jnp.concatenate([last_tile, unused]).astype(jnp.int32)
    return dest, tile_expert, n_used.astype(jnp.int32), clear_tiles, n_tiles


def _by_tile(dest, t, tb):
    return dest.reshape(t // tb, tb, 2).transpose(0, 2, 1)


def kernel(x, c, l0_ada_w, l0_ada_b, l0_norm1_g, l0_norm2_g, l0_ret_w_in, l0_ret_decay_logit, l0_ret_gn_g, l0_ret_w_out, l0_ffn_w_gate, l0_ffn_w_up, l0_ffn_w_down, l1_ada_w, l1_ada_b, l1_norm1_g, l1_norm2_g, l1_attn_w_qkv, l1_attn_q_norm_g, l1_attn_k_norm_g, l1_attn_w_out, l1_moe_w_router, l1_moe_b_router, l1_moe_w_gate, l1_moe_w_up, l1_moe_w_down, final_ada_w, final_ada_b, final_norm_g):
    bsz, seq, d = x.shape
    t = bsz * seq
    xf = x.reshape(t, d)

    c8 = jnp.zeros((8, d), F32).at[:bsz].set(c)

    def mods(w, b, n):
        m = _adaln(c8, w, b)[:bsz]
        return [m[:, i * d:(i + 1) * d].reshape(bsz, 1, d) for i in range(n)]

    sh1, sc1, g1, sh2, sc2, g2 = mods(l0_ada_w, l0_ada_b, 6)
    sh3, sc3, g3, sh4, sc4, g4 = mods(l1_ada_w, l1_ada_b, 6)
    fsh, fsc = mods(final_ada_w, final_ada_b, 2)

    bf = lambda w: w.astype(BF16)

    proj = _inproj0(xf, l0_norm1_g, sh1, sc1, bf(l0_ret_w_in), seq)
    log_gamma = jax.nn.log_sigmoid(l0_ret_decay_logit.astype(F32))
    u = _retention(proj, log_gamma, l0_ret_gn_g, bsz, seq)
    x1, h2 = _outproj(u, bf(l0_ret_w_out), xf, g1, l0_norm2_g, sh2, sc2, seq)
    x2, h3 = _ffn0(h2, bf(l0_ffn_w_gate), bf(l0_ffn_w_up), bf(l0_ffn_w_down), x1, g2, l1_norm1_g, sh3, sc3, seq)

    qkv, norm2 = _qkv1(h3, bf(l1_attn_w_qkv), l1_attn_q_norm_g, l1_attn_k_norm_g, seq)
    o = _attention(qkv, norm2, bsz, seq)
    wr = jnp.zeros((d, LANES), F32).at[:, :N_EXPERTS].set(l1_moe_w_router)
    wr_hi = wr.astype(BF16)
    wr_lo = (wr - wr_hi.astype(F32)).astype(BF16)
    br = jnp.full((1, LANES), -jnp.inf, F32).at[0, :N_EXPERTS].set(l1_moe_b_router.astype(F32))
    x3, h4, idx, rw = _outproj(o, bf(l1_attn_w_out), x2, g3, l1_norm2_g, sh4, sc4, seq,
                               router=(jnp.concatenate([wr_hi, wr_lo], axis=1), br), h_dtype=F32)
    dest, tile_expert, n_used, clear_tiles, n_tiles = _moe_plan(idx[:, :2], t)
    xs = _dispatch(h4, _by_tile(dest, t, DISPATCH_TILE), clear_tiles, n_tiles)
    ys = _moe_ffn(xs, tile_expert, n_used, bf(l1_moe_w_gate), bf(l1_moe_w_up), bf(l1_moe_w_down))
    out = _combine(_by_tile(dest, t, COMBINE_TILE), ys, x3, rw, g4, final_norm_g, fsh, fsc, seq)
    return out.reshape(bsz, seq, d)
```

```python
import functools

import numpy as np
import jax
import jax.numpy as jnp
from jax import lax
from jax.experimental import pallas as pl
from jax.experimental.pallas import tpu as pltpu

GRID_W = 64
NORM_EPS = 1e-6
ROPE_THETA = 10000.0
RET_HEADS = 4
RET_DK = 256
RET_DV = 512
RET_CHUNK = 128
ATT_HEADS = 8
ATT_KV_HEADS = 2
ATT_HD = 128
ATT_GROUP = ATT_HEADS // ATT_KV_HEADS
N_EXPERTS = 8
LANES = 128
LOG2_E = 1.4426950408889634

F32 = jnp.float32
BF16 = jnp.bfloat16
HIGHEST = lax.Precision.HIGHEST

TOKEN_TILE = 512
RET_BLOCK = 512
ATT_Q_TILE_BOUNDED = 512
ATT_Q_TILE_ONLINE = 256
ATT_K_TILE = 1024
ATT_SLOTS = 4
ATT_ONES_ROWS = 16
ATT_BOUND_LIMIT = 60.0
MOE_TILE = 512
MOE_FF_TILE = 1792
DISPATCH_TILE = 512
COMBINE_TILE = 256
VMEM_LIMIT = 56 * 1024 * 1024


def _params(**kw):
    return pltpu.CompilerParams(vmem_limit_bytes=VMEM_LIMIT, **kw)


def _resident(shape):
    return pl.BlockSpec(shape, lambda *_: (0,) * len(shape), pipeline_mode=pl.Buffered(1))


def _silu(a):
    return a * jax.nn.sigmoid(a)


def _rms_mod(x, g, shift, scale):
    y = x * lax.rsqrt(jnp.mean(x * x, axis=-1, keepdims=True) + NORM_EPS)
    return (y * g) * (1.0 + scale) + shift


def _adaln_kernel(c_ref, w_ref, b_ref, o_ref):
    a = _silu(c_ref[...])
    o_ref[...] = jnp.dot(a, w_ref[...], precision=HIGHEST, preferred_element_type=F32) + b_ref[...]


def _adaln(c8, w, b):
    d, n = w.shape
    tn = 1024
    return pl.pallas_call(
        _adaln_kernel,
        grid=(n // tn,),
        in_specs=[
            pl.BlockSpec((8, d), lambda j: (0, 0)),
            pl.BlockSpec((d, tn), lambda j: (0, j)),
            pl.BlockSpec((1, tn), lambda j: (0, j)),
        ],
        out_specs=pl.BlockSpec((8, tn), lambda j: (0, j)),
        out_shape=jax.ShapeDtypeStruct((8, n), F32),
        compiler_params=_params(),
        name="adaln",
    )(c8, w, b.reshape(1, n))


def _rope_tables(seq, half):
    nfreq = half // 2
    freqs = ROPE_THETA ** (-np.arange(0, nfreq, dtype=np.float64) * 2.0 / half)
    npos = max(seq // GRID_W, GRID_W)
    ang = np.arange(npos, dtype=np.float64)[:, None] * freqs[None, :]
    cos = np.concatenate([np.cos(ang), np.cos(ang)], axis=1)
    sin = np.concatenate([-np.sin(ang), np.sin(ang)], axis=1)
    return cos.astype(np.float32), sin.astype(np.float32)


def _inproj0_kernel(x_ref, g_ref, sh_ref, sc_ref, w_ref, cr_ref, sr_ref, cc_ref, scol_ref, o_ref, *, tm):
    h = _rms_mod(x_ref[...], g_ref[...], sh_ref[...], sc_ref[...]).astype(BF16)
    rows = tm // GRID_W
    cr = cr_ref[...]
    sr = sr_ref[...]
    cc = cc_ref[...][None]
    scol = scol_ref[...][None]
    nw = 512
    n_qk = 2 * RET_HEADS * RET_DK
    for c in range(o_ref.shape[1] // nw):
        y = jnp.dot(h, w_ref[:, c * nw:(c + 1) * nw], preferred_element_type=F32)
        if c * nw < n_qk:
            is_k = c * nw >= RET_HEADS * RET_DK
            for s in range(nw // LANES):
                ys = y[:, s * LANES:(s + 1) * LANES]
                y3 = ys.reshape(rows, GRID_W, LANES)
                r3 = pltpu.roll(ys, LANES // 2, 1).reshape(rows, GRID_W, LANES)
                if s % 2 == 0:
                    o3 = y3 * cr + r3 * sr
                else:
                    o3 = y3 * cc + r3 * scol
                if is_k:
                    o3 = o3 * (RET_DK ** -0.5)
                o_ref[:, c * nw + s * LANES:c * nw + (s + 1) * LANES] = o3.reshape(tm, LANES).astype(BF16)
        else:
            o_ref[:, c * nw:(c + 1) * nw] = y.astype(BF16)


def _inproj0(x, g, sh, sc, w_bf16, seq):
    t, d = x.shape
    n = w_bf16.shape[1]
    tm = TOKEN_TILE
    per_b = seq // tm
    rows = tm // GRID_W
    cos, sin = _rope_tables(seq, LANES)
    n_img_rows = seq // GRID_W
    cr = jnp.asarray(cos[:n_img_rows].reshape(n_img_rows, 1, LANES))
    sr = jnp.asarray(sin[:n_img_rows].reshape(n_img_rows, 1, LANES))
    cc = jnp.asarray(cos[:GRID_W])
    scol = jnp.asarray(sin[:GRID_W])
    vec = lambda: pl.BlockSpec((1, d), lambda i: (0, 0))
    bvec = lambda: pl.BlockSpec((None, 1, d), lambda i: (i // per_b, 0, 0))
    rowt = lambda: pl.BlockSpec((rows, 1, LANES), lambda i: (i % per_b, 0, 0))
    colt = lambda: pl.BlockSpec((GRID_W, LANES), lambda i: (0, 0))
    return pl.pallas_call(
        functools.partial(_inproj0_kernel, tm=tm),
        grid=(t // tm,),
        in_specs=[pl.BlockSpec((tm, d), lambda i: (i, 0)), vec(), bvec(), bvec(),
                  _resident((d, n)), rowt(), rowt(), colt(), colt()],
        out_specs=pl.BlockSpec((tm, n), lambda i: (i, 0)),
        out_shape=jax.ShapeDtypeStruct((t, n), BF16),
        compiler_params=_params(),
        name="l0_inproj",
    )(x, g.reshape(1, d), sh, sc, w_bf16, cr, sr, cc, scol)


def _col(n):
    return lax.broadcasted_iota(jnp.int32, (n, 1), 0).astype(F32)


def _ret_fwd_kernel(lg_ref, q_ref, k_ref, v_ref, y_ref, state_ref, *, nchunks):
    @pl.when(pl.program_id(1) == 0)
    def _():
        state_ref[...] = jnp.zeros_like(state_ref)

    c_len = RET_CHUNK
    ii = lax.broadcasted_iota(jnp.int32, (c_len, c_len), 0)
    jj = lax.broadcasted_iota(jnp.int32, (c_len, c_len), 1)
    diff = (ii - jj).astype(F32)
    idx = _col(c_len)
    consts = []
    for hd in range(RET_HEADS):
        lgf = lg_ref[0, hd]
        lgb = lg_ref[1, hd]
        dmat = jnp.where(diff >= 0, jnp.exp(diff * lgf), jnp.exp(-diff * lgb))
        xi = jnp.exp((idx + 1.0) * lgf)
        zeta = jnp.exp((c_len - 1.0 - idx) * lgf)
        decay = jnp.exp(jnp.full((1, 1), c_len, F32) * lgf)
        consts.append((dmat, xi, zeta, decay))
    for c in range(nchunks):
        sl = slice(c * c_len, (c + 1) * c_len)
        for hd in range(RET_HEADS):
            dmat, xi, zeta, decay = consts[hd]
            qk = slice(hd * RET_DK, (hd + 1) * RET_DK)
            vv = slice(hd * RET_DV, (hd + 1) * RET_DV)
            q = q_ref[sl, qk]
            k = k_ref[sl, qk]
            v = v_ref[sl, vv]
            s = lax.dot_general(q, k, (((1,), (1,)), ((), ())), preferred_element_type=F32) * dmat
            y = jnp.dot(s.astype(BF16), v, preferred_element_type=F32)
            st = state_ref[hd]
            y = y + jnp.dot(q, st.astype(BF16), preferred_element_type=F32) * xi
            kz = (k.astype(F32) * zeta).astype(BF16)
            upd = lax.dot_general(kz, v, (((0,), (0,)), ((), ())), preferred_element_type=F32)
            state_ref[hd] = st * decay + upd
            y_ref[sl, vv] = y.astype(y_ref.dtype)


def _ret_bwd_kernel(lg_ref, q_ref, k_ref, v_ref, g_ref, yf_ref, gn_ref, u_ref, state_ref, *, nchunks):
    @pl.when(pl.program_id(1) == 0)
    def _():
        state_ref[...] = jnp.zeros_like(state_ref)

    c_len = RET_CHUNK
    idx = _col(c_len)
    consts = []
    for hd in range(RET_HEADS):
        lgb = lg_ref[1, hd]
        consts.append((jnp.exp((c_len - idx) * lgb), jnp.exp(idx * lgb),
                       jnp.exp(jnp.full((1, 1), c_len, F32) * lgb)))
    for c in reversed(range(nchunks)):
        sl = slice(c * c_len, (c + 1) * c_len)
        for hd in range(RET_HEADS):
            xi, zeta, decay = consts[hd]
            qk = slice(hd * RET_DK, (hd + 1) * RET_DK)
            vv = slice(hd * RET_DV, (hd + 1) * RET_DV)
            q = q_ref[sl, qk]
            k = k_ref[sl, qk]
            v = v_ref[sl, vv]
            st = state_ref[hd]
            y = yf_ref[sl, vv].astype(F32) + jnp.dot(q, st.astype(BF16), preferred_element_type=F32) * xi
            kz = (k.astype(F32) * zeta).astype(BF16)
            upd = lax.dot_general(kz, v, (((0,), (0,)), ((), ())), preferred_element_type=F32)
            state_ref[hd] = st * decay + upd
            mu = jnp.mean(y, axis=-1, keepdims=True)
            yc = y - mu
            var = jnp.mean(yc * yc, axis=-1, keepdims=True)
            yn = yc * lax.rsqrt(var + NORM_EPS) * gn_ref[:, vv]
            u_ref[sl, vv] = (_silu(g_ref[sl, vv].astype(F32)) * yn).astype(u_ref.dtype)


def _retention(proj, log_gamma, gn_g, bsz, seq):
    t = proj.shape[0]
    tb = RET_BLOCK
    nb = seq // tb
    nchunks = tb // RET_CHUNK
    n_qk = RET_HEADS * RET_DK
    n_v = RET_HEADS * RET_DV
    assert n_v == 2 * n_qk
    grid = (bsz, nb)
    qspec = lambda f: pl.BlockSpec((tb, n_qk), lambda b, n, lg: (b * nb + f(n), 0))
    kspec = lambda f: pl.BlockSpec((tb, n_qk), lambda b, n, lg: (b * nb + f(n), 1))
    vspec = lambda f: pl.BlockSpec((tb, n_v), lambda b, n, lg: (b * nb + f(n), 1))
    gspec = lambda f: pl.BlockSpec((tb, n_v), lambda b, n, lg: (b * nb + f(n), 2))
    yspec = lambda f: pl.BlockSpec((tb, n_v), lambda b, n, lg: (b * nb + f(n), 0))
    fwd = lambda n: n
    rev = lambda n: nb - 1 - n
    state = pltpu.VMEM((RET_HEADS, RET_DK, RET_DV), F32)
    yf = pl.pallas_call(
        functools.partial(_ret_fwd_kernel, nchunks=nchunks),
        grid_spec=pltpu.PrefetchScalarGridSpec(
            num_scalar_prefetch=1, grid=grid,
            in_specs=[qspec(fwd), kspec(fwd), vspec(fwd)],
            out_specs=yspec(fwd), scratch_shapes=[state]),
        out_shape=jax.ShapeDtypeStruct((t, RET_HEADS * RET_DV), BF16),
        compiler_params=_params(),
        name="l0_ret_fwd",
    )(log_gamma, proj, proj, proj)
    return pl.pallas_call(
        functools.partial(_ret_bwd_kernel, nchunks=nchunks),
        grid_spec=pltpu.PrefetchScalarGridSpec(
            num_scalar_prefetch=1, grid=grid,
            in_specs=[qspec(rev), kspec(rev), vspec(rev), gspec(rev), yspec(rev),
                      pl.BlockSpec((1, n_v), lambda b, n, lg: (0, 0))],
            out_specs=yspec(rev), scratch_shapes=[state]),
        out_shape=jax.ShapeDtypeStruct((t, RET_HEADS * RET_DV), BF16),
        compiler_params=_params(),
        name="l0_ret_bwd",
    )(log_gamma, proj, proj, proj, proj, yf, gn_g.reshape(1, -1))


def _top2(logits):
    lane = lax.broadcasted_iota(jnp.int32, logits.shape, 1)
    v1 = jnp.max(logits, axis=-1, keepdims=True)
    i1 = jnp.min(jnp.where(logits == v1, lane, LANES), axis=-1, keepdims=True)
    rest = jnp.where(lane == i1, -jnp.inf, logits)
    v2 = jnp.max(rest, axis=-1, keepdims=True)
    i2 = jnp.min(jnp.where(rest == v2, lane, LANES), axis=-1, keepdims=True)
    e = jnp.exp(v2 - v1)
    w1 = 1.0 / (1.0 + e)
    w2 = e / (1.0 + e)
    idx = jnp.where(lane == 0, i1, jnp.where(lane == 1, i2, 0))
    w = jnp.where(lane == 0, w1, jnp.where(lane == 1, w2, 0.0))
    return idx, w


def _outproj_kernel(u_ref, w_ref, x_ref, gate_ref, g_ref, sh_ref, sc_ref, *rest, router):
    if router:
        wr_ref, br_ref, xo_ref, ho_ref, idx_ref, rw_ref = rest
    else:
        xo_ref, ho_ref = rest
    m = jnp.dot(u_ref[...], w_ref[...], preferred_element_type=F32)
    x1 = x_ref[...] + gate_ref[...] * m
    xo_ref[...] = x1
    h = _rms_mod(x1, g_ref[...], sh_ref[...], sc_ref[...])
    ho_ref[...] = h.astype(ho_ref.dtype)
    if router:
        h_hi = h.astype(BF16)
        h_lo = (h - h_hi.astype(F32)).astype(BF16)
        a = jnp.dot(h_hi, wr_ref[...], preferred_element_type=F32)
        b = jnp.dot(h_lo, wr_ref[:, :LANES], preferred_element_type=F32)
        logits = (a[:, :LANES] + a[:, LANES:]) + b + br_ref[...]
        idx, w = _top2(logits)
        idx_ref[...] = idx
        rw_ref[...] = w


def _outproj(u, w_bf16, x, gate, g, sh, sc, seq, router=None, h_dtype=BF16):
    t, d = x.shape
    kdim = u.shape[1]
    tm = TOKEN_TILE
    per_b = seq // tm
    row = lambda width: pl.BlockSpec((tm, width), lambda i: (i, 0))
    vec = lambda: pl.BlockSpec((1, d), lambda i: (0, 0))
    bvec = lambda: pl.BlockSpec((None, 1, d), lambda i: (i // per_b, 0, 0))
    in_specs = [row(kdim), _resident((kdim, d)), row(d), bvec(), vec(), bvec(), bvec()]
    args = [u, w_bf16, x, gate, g.reshape(1, d), sh, sc]
    out_specs = [row(d), row(d)]
    out_shape = [jax.ShapeDtypeStruct((t, d), F32), jax.ShapeDtypeStruct((t, d), h_dtype)]
    if router is not None:
        wr, br = router
        in_specs += [_resident((d, 2 * LANES)), pl.BlockSpec((1, LANES), lambda i: (0, 0))]
        args += [wr, br]
        out_specs += [row(LANES), row(LANES)]
        out_shape += [jax.ShapeDtypeStruct((t, LANES), jnp.int32), jax.ShapeDtypeStruct((t, LANES), F32)]
    return pl.pallas_call(
        functools.partial(_outproj_kernel, router=router is not None),
        grid=(t // tm,),
        in_specs=in_specs, out_specs=out_specs, out_shape=out_shape,
        compiler_params=_params(),
        name="outproj_router" if router is not None else "outproj",
    )(*args)


def _ffn0_kernel(h_ref, wg_ref, wu_ref, wd_ref, x_ref, gate_ref, g_ref, sh_ref, sc_ref, xo_ref, ho_ref, *, tf):
    h = h_ref[...]
    acc = jnp.zeros(x_ref.shape, F32)
    for c in range(wg_ref.shape[1] // tf):
        cs = slice(c * tf, (c + 1) * tf)
        a = jnp.dot(h, wg_ref[:, cs], preferred_element_type=F32)
        b = jnp.dot(h, wu_ref[:, cs], preferred_element_type=F32)
        act = (_silu(a) * b).astype(BF16)
        acc = acc + jnp.dot(act, wd_ref[cs, :], preferred_element_type=F32)
    x2 = x_ref[...] + gate_ref[...] * acc
    xo_ref[...] = x2
    ho_ref[...] = _rms_mod(x2, g_ref[...], sh_ref[...], sc_ref[...]).astype(ho_ref.dtype)


def _ffn0(h, wg, wu, wd, x, gate, g, sh, sc, seq):
    t, d = x.shape
    f = wg.shape[1]
    tm = TOKEN_TILE
    per_b = seq // tm
    row = lambda: pl.BlockSpec((tm, d), lambda i: (i, 0))
    vec = lambda: pl.BlockSpec((1, d), lambda i: (0, 0))
    bvec = lambda: pl.BlockSpec((None, 1, d), lambda i: (i // per_b, 0, 0))
    return pl.pallas_call(
        functools.partial(_ffn0_kernel, tf=f),
        grid=(t // tm,),
        in_specs=[row(), _resident((d, f)), _resident((d, f)), _resident((f, d)), row(), bvec(), vec(), bvec(), bvec()],
        out_specs=[row(), row()],
        out_shape=[jax.ShapeDtypeStruct((t, d), F32), jax.ShapeDtypeStruct((t, d), BF16)],
        compiler_params=_params(),
        name="l0_ffn",
    )(h, wg, wu, wd, x, gate, g.reshape(1, d), sh, sc)


def _qkv1_kernel(h_ref, w_ref, qg_ref, kg_ref, cr_ref, s1r_ref, s2r_ref, cc_ref, s1c_ref, s2c_ref, o_ref, n2_ref,
                 *, tm):
    rows = tm // GRID_W
    y = jnp.dot(h_ref[...], w_ref[...], preferred_element_type=F32)
    cos =(cr_ref[...] + cc_ref[...][None]).reshape(tm, LANES)
    s1 = (s1r_ref[...] + s1c_ref[...][None]).reshape(tm, LANES)
    s2 = (s2r_ref[...] + s2c_ref[...][None]).reshape(tm, LANES)
    n_rot = ATT_HEADS + ATT_KV_HEADS
    lane = lax.broadcasted_iota(jnp.int32, (tm, LANES), 1)
    norm2 = jnp.zeros((tm, LANES), F32)
    ones = jnp.ones((LANES, LANES), BF16)

    def row_sums(a):
        a_hi = a.astype(BF16)
        a_lo = (a - a_hi.astype(F32)).astype(BF16)
        return (jnp.dot(a_hi, ones, preferred_element_type=F32) + jnp.dot(a_lo, ones, preferred_element_type=F32))

    for j in range(n_rot):
        ys = y[:, j * LANES:(j + 1) * LANES]
        gain = qg_ref[...] if j < ATT_HEADS else kg_ref[...]
        yn = ys * lax.rsqrt(row_sums(ys * ys) * (1.0 / LANES) + NORM_EPS) * gain
        o = yn * cos + pltpu.roll(yn, 96, 1) * s1 + pltpu.roll(yn, 32, 1) * s2
        if j < ATT_HEADS:
            o = o * (ATT_HD ** -0.5 * LOG2_E)
        ob = o.astype(BF16)
        o_ref[:, j * LANES:(j + 1) * LANES] = ob
        norm2 = jnp.where(lane == j, row_sums(o * o), norm2)
    o_ref[:, n_rot * LANES:] = y[:, n_rot * LANES:].astype(BF16)
    n2_ref[...] = norm2


def _att_rope_tables(seq):
    half = ATT_HD // 2
    nfreq = half // 2
    freqs = ROPE_THETA ** (-np.arange(0, nfreq, dtype=np.float64) * 2.0 / half)
    lane = np.arange(LANES)
    f_l = freqs[lane % nfreq]
    first = (lane % half) < nfreq
    is_row = lane < half
    n_img_rows = seq // GRID_W

    def tables(npos, mask):
        ang = np.arange(npos, dtype=np.float64)[:, None] * f_l[None, :]
        c = np.where(mask[None], np.cos(ang), 0.0)
        s1 = np.where((mask & first)[None], -np.sin(ang), 0.0)
        s2 = np.where((mask & ~first)[None], np.sin(ang), 0.0)
        return [a.astype(np.float32) for a in (c, s1, s2)]

    row_t = [jnp.asarray(a.reshape(n_img_rows, 1, LANES)) for a in tables(n_img_rows, is_row)]
    col_t = [jnp.asarray(a) for a in tables(GRID_W, ~is_row)]
    return row_t, col_t


def _qkv1(h, w_bf16, qg, kg, seq):
    t, d = h.shape
    n = w_bf16.shape[1]
    tm = TOKEN_TILE
    per_b = seq // tm
    rows = tm // GRID_W
    row_t, col_t = _att_rope_tables(seq)
    rowt = lambda: pl.BlockSpec((rows, 1, LANES), lambda i: (i % per_b, 0, 0))
    colt = lambda: pl.BlockSpec((GRID_W, LANES), lambda i: (0, 0))
    hvec = lambda: pl.BlockSpec((1, LANES), lambda i: (0, 0))
    return pl.pallas_call(
        functools.partial(_qkv1_kernel, tm=tm),
        grid=(t // tm,),
        in_specs=[pl.BlockSpec((tm, d), lambda i: (i, 0)), _resident((d, n)), hvec(), hvec(),
                  rowt(), rowt(), rowt(), colt(), colt(), colt()],
        out_specs=[pl.BlockSpec((tm, n), lambda i: (i, 0)), pl.BlockSpec((tm, LANES), lambda i: (i, 0))],
        out_shape=[jax.ShapeDtypeStruct((t, n), BF16), jax.ShapeDtypeStruct((t, LANES), F32)],
        compiler_params=_params(),
        name="l1_qkv",
    )(h, w_bf16, qg.reshape(1, LANES), kg.reshape(1, LANES), *row_t, *col_t)


def _stack_heads(q_ref, qs_ref, tq):
    for g in range(ATT_GROUP):
        qs_ref[g * tq:(g + 1) * tq, :] = q_ref[:, g * LANES:(g + 1) * LANES]


def _pipelined(nk, scores, update):
    scores(0, 0)
    scores(1, 1)

    def body(jj, carry):
        j = ATT_SLOTS * jj
        for u in range(ATT_SLOTS):
            scores(j + u + 2, (u + 2) % ATT_SLOTS)
            update(j + u, u)
        return carry

    lax.fori_loop(0, nk // ATT_SLOTS - 1, body, 0)
    j = nk - ATT_SLOTS
    for u in range(ATT_SLOTS):
        if u + 2 < ATT_SLOTS:
            scores(j + u + 2, (u + 2) % ATT_SLOTS)
        update(j + u, u)


def _store_heads(o, o_ref, tq):
    for g in range(ATT_GROUP):
        o_ref[:, g * LANES:(g + 1) * LANES] = o[:, g * tq:(g + 1) * tq].T.astype(o_ref.dtype)


def _flash_bounded_kernel(q_ref, k_ref, vt_ref, mb_ref, o_ref, qs_ref, acc_ref, p_ref, vte_ref, *, tq, tk):
    nk = vt_ref.shape[0]

    @pl.when(pl.program_id(2) == 0)
    def _():
        for j in range(nk):
            vte_ref[j, :ATT_HD, :] = vt_ref[j]
            vte_ref[j, ATT_HD:, :] = jnp.ones((ATT_ONES_ROWS, tk), BF16)

    _stack_heads(q_ref, qs_ref, tq)
    acc_ref[...] = jnp.zeros_like(acc_ref)
    mb = mb_ref[...]

    def scores(j, slot):
        k_c = k_ref[pl.ds(pl.multiple_of(j * tk, tk), tk), :]
        s = lax.dot_general(k_c, qs_ref[...], (((1,), (1,)), ((), ())), preferred_element_type=F32)
        p_ref[slot] = jnp.exp2(s - mb).astype(BF16)

    def update(j, slot):
        acc_ref[...] += jnp.dot(vte_ref[j], p_ref[slot], preferred_element_type=F32)

    _pipelined(nk, scores, update)
    _store_heads(acc_ref[:ATT_HD, :] / acc_ref[ATT_HD:ATT_HD + 1, :], o_ref, tq)


def _flash_online_kernel(q_ref, k_ref, vt_ref, o_ref, qs_ref, m_ref, l_ref, acc_ref, s_ref, cm_ref, *, tq, tk):
    _stack_heads(q_ref, qs_ref, tq)
    m_ref[...] = jnp.full(m_ref.shape, -jnp.inf, F32)
    l_ref[...] = jnp.zeros_like(l_ref)
    acc_ref[...] = jnp.zeros_like(acc_ref)
    nk = vt_ref.shape[0]

    def scores(j, slot):
        k_c = k_ref[pl.ds(pl.multiple_of(j * tk, tk), tk), :]
        s = lax.dot_general(k_c, qs_ref[...], (((1,), (1,)), ((), ())), preferred_element_type=F32)
        s_ref[slot] = s
        cm_ref[slot] = jnp.max(s, axis=0, keepdims=True)

    def update(j, slot):
        m_old = m_ref[...]
        m_new = jnp.maximum(m_old, cm_ref[slot])
        p = jnp.exp2(s_ref[slot] - m_new)
        alpha = jnp.exp2(m_old - m_new)
        l_ref[...] = alpha * l_ref[...] + jnp.sum(p, axis=0, keepdims=True)
        acc_ref[...] = alpha * acc_ref[...] + jnp.dot(vt_ref[j], p.astype(BF16), preferred_element_type=F32)
        m_ref[...] = m_new

    _pipelined(nk, scores, update)
    _store_heads(acc_ref[...] / l_ref[...], o_ref, tq)


def _attention(qkv, norm2, bsz, seq):
    t = qkv.shape[0]
    tk = min(ATT_K_TILE, seq // ATT_SLOTS)
    nk = seq // tk
    assert nk % ATT_SLOTS == 0 and nk * tk == seq, "the score pipeline rotates through ATT_SLOTS key chunks"
    gw = ATT_GROUP * ATT_HD
    v = qkv[:, (ATT_HEADS + ATT_KV_HEADS) * ATT_HD:]
    vt = v.reshape(bsz, nk, tk, ATT_KV_HEADS, ATT_HD).transpose(0, 3, 1, 4, 2)

    tqb = min(ATT_Q_TILE_BOUNDED, seq)
    nqb = seq // tqb
    qn = jnp.sqrt(norm2[:, :ATT_HEADS]).reshape(bsz, nqb, tqb, ATT_KV_HEADS, ATT_GROUP)
    kn = jnp.sqrt(jnp.max(norm2[:, ATT_HEADS:ATT_HEADS + ATT_KV_HEADS].reshape(bsz, seq, ATT_KV_HEADS), axis=1))
    mb = qn.transpose(0, 3, 1, 4, 2) * kn[:, :, None, None, None]
    mb = mb.reshape(bsz * ATT_KV_HEADS * nqb, 1, ATT_GROUP * tqb)

    def call(body, tq, scratch, bound=None):
        nq = seq // tq
        nqs = ATT_GROUP * tq
        in_specs = [
            pl.BlockSpec((tq, gw), lambda b, kv, i: (b * nq + i, kv)),
            pl.BlockSpec((seq, ATT_HD), lambda b, kv, i: (b, ATT_HEADS + kv)),
            pl.BlockSpec((None, None, nk, ATT_HD, tk), lambda b, kv, i: (b, kv, 0, 0, 0)),
        ]
        args = [qkv, qkv, vt]
        if bound is not None:
            in_specs.append(pl.BlockSpec((None, 1, nqs), lambda b, kv, i: ((b * ATT_KV_HEADS + kv) * nq + i, 0, 0)))
            args.append(bound)
        return pl.pallas_call(
            functools.partial(body, tq=tq, tk=tk),
            grid=(bsz, ATT_KV_HEADS, nq),
            in_specs=in_specs,
            out_specs=pl.BlockSpec((tq, gw), lambda b, kv, i: (b * nq + i, kv)),
            out_shape=jax.ShapeDtypeStruct((t, ATT_HEADS * ATT_HD), BF16),
            scratch_shapes=[pltpu.VMEM((nqs, ATT_HD), BF16)] + scratch(nqs),
            compiler_params=_params(),
            name="l1_attention",
        )(*args)

    def bounded():
        return call(_flash_bounded_kernel, tqb,
                    lambda nqs: [pltpu.VMEM((ATT_HD + ATT_ONES_ROWS, nqs), F32), pltpu.VMEM((ATT_SLOTS, tk, nqs), BF16),
                                 pltpu.VMEM((nk, ATT_HD + ATT_ONES_ROWS, tk), BF16)], bound=mb)

    def online():
        return call(_flash_online_kernel, min(ATT_Q_TILE_ONLINE, seq),
                    lambda nqs: [pltpu.VMEM((1, nqs), F32), pltpu.VMEM((1, nqs), F32), pltpu.VMEM((ATT_HD, nqs), F32),
                                 pltpu.VMEM((ATT_SLOTS, tk, nqs), F32), pltpu.VMEM((ATT_SLOTS, 1, nqs), F32)])

    return lax.cond(jnp.max(mb) <= ATT_BOUND_LIMIT, bounded, online)


def _dispatch_kernel(lt_ref, dest_ref, h_ref, xs_hbm, zero_ref, zsem, sem, *, tb, tm):
    @pl.when(pl.program_id(0) == 0)
    def _():
        zero_ref[...] = jnp.zeros_like(zero_ref)
        for e in range(2 * N_EXPERTS):
            fill = pltpu.make_async_copy(zero_ref, xs_hbm.at[pl.ds(lt_ref[e] * tm, tm), :], zsem)
            fill.start()
            fill.wait()

    for r in range(tb):
        for k in range(2):
            pltpu.make_async_copy(h_ref.at[pl.ds(r, 1), :], xs_hbm.at[pl.ds(dest_ref[k, r], 1), :],
                                  sem).start(priority=k)
    for k in range(2):
        pltpu.make_async_copy(h_ref, xs_hbm.at[pl.ds(0, tb), :], sem).wait()


def _dispatch(h, dest, last_tile, n_tiles):
    t, d = h.shape
    tb = DISPATCH_TILE
    tm = MOE_TILE
    return pl.pallas_call(
        functools.partial(_dispatch_kernel, tb=tb, tm=tm),
        grid_spec=pltpu.PrefetchScalarGridSpec(
            num_scalar_prefetch=1, grid=(t // tb,),
            in_specs=[
                pl.BlockSpec((None, 2, tb), lambda i, lt: (i, 0, 0), memory_space=pltpu.SMEM),
                pl.BlockSpec((tb, d), lambda i, lt: (i, 0)),
            ],
            out_specs=pl.BlockSpec(memory_space=pl.ANY),
            scratch_shapes=[pltpu.VMEM((tm, d), F32), pltpu.SemaphoreType.DMA(()), pltpu.SemaphoreType.DMA(())]),
        out_shape=jax.ShapeDtypeStruct((n_tiles * tm, d), F32),
        compiler_params=_params(),
        name="l1_moe_dispatch",
    )(last_tile, dest, h)


def _moe_ffn_kernel(te_ref, nu_ref, x_ref, wg_ref, wu_ref, wd_ref, y_ref, xb_ref, acc_ref):
    i = pl.program_id(0)
    j = pl.program_id(1)
    nj = pl.num_programs(1)
    used = i < nu_ref[0]

    @pl.when(jnp.logical_and(used, j == 0))
    def _():
        xb_ref[...] = x_ref[...].astype(BF16)

    @pl.when(used)
    def _():
        xb = xb_ref[...]
        a = jnp.dot(xb, wg_ref[...], preferred_element_type=F32)
        b = jnp.dot(xb, wu_ref[...], preferred_element_type=F32)
        act = (_silu(a) * b).astype(BF16)
        part = jnp.dot(act, wd_ref[...], preferred_element_type=F32)

        @pl.when(j == 0)
        def _():
            acc_ref[...] = part

        @pl.when(jnp.logical_and(j > 0, j < nj - 1))
        def _():
            acc_ref[...] += part

        @pl.when(j == nj - 1)
        def _():
            y_ref[...] = acc_ref[...] + part

    @pl.when(jnp.logical_and(jnp.logical_not(used), j == nj - 1))
    def _():
        y_ref[...] = jnp.zeros_like(y_ref)


def _moe_ffn(xs, tile_expert, n_used, wg, wu, wd):
    d = xs.shape[1]
    tm = MOE_TILE
    n_tiles = xs.shape[0] // tm
    tf = MOE_FF_TILE
    f = wg.shape[2]
    nj = f // tf
    assert nj >= 2 and nj * tf == f

    def jeff(i, j, nu):
        return jnp.where(i < nu[0], j, nj - 1)

    return pl.pallas_call(
        _moe_ffn_kernel,
        grid_spec=pltpu.PrefetchScalarGridSpec(
            num_scalar_prefetch=2, grid=(n_tiles, nj),
            in_specs=[
                pl.BlockSpec((tm, d), lambda i, j, te, nu: (jnp.minimum(i, nu[0] - 1), 0)),
                pl.BlockSpec((None, d, tf), lambda i, j, te, nu: (te[i], 0, jeff(i, j, nu))),
                pl.BlockSpec((None, d, tf), lambda i, j, te, nu: (te[i], 0, jeff(i, j, nu))),
                pl.BlockSpec((None, tf, d), lambda i, j, te, nu: (te[i], jeff(i, j, nu), 0)),
            ],
            out_specs=pl.BlockSpec((tm, d), lambda i, j, te, nu: (i, 0)),
            scratch_shapes=[pltpu.VMEM((tm, d), BF16), pltpu.VMEM((tm, d), F32)]),
        out_shape=jax.ShapeDtypeStruct((n_tiles * tm, d), F32),
        compiler_params=_params(),
        name="l1_moe_ffn",
    )(tile_expert, n_used, xs, wg, wu, wd)


def _combine_kernel(dfirst_ref, dnext_ref, y_hbm, x_ref, rw_ref, gate_ref, g_ref, sh_ref, sc_ref, o_ref,
                    ya_ref, yb_ref, sem, *, tb):
    i = pl.program_id(0)
    n = pl.num_programs(0)
    slot = i % 2

    def row_copies(dest_ref, r, buf_slot):
        ca = pltpu.make_async_copy(y_hbm.at[pl.ds(dest_ref[0, r], 1), :], ya_ref.at[buf_slot, pl.ds(r, 1), :],
                                   sem.at[buf_slot])
        cb = pltpu.make_async_copy(y_hbm.at[pl.ds(dest_ref[1, r], 1), :], yb_ref.at[buf_slot, pl.ds(r, 1), :],
                                   sem.at[buf_slot])
        return ca, cb

    @pl.when(i == 0)
    def _():
        def issue(r, c):
            ca, cb = row_copies(dfirst_ref, r, 0)
            ca.start()
            cb.start()
            return c
        lax.fori_loop(0, tb, issue, 0)

    @pl.when(i + 1 < n)
    def _():
        for r in range(tb):
            ca, cb = row_copies(dnext_ref, r, 1 - slot)
            ca.start(priority=0)
            cb.start(priority=1)

    pltpu.make_async_copy(y_hbm.at[pl.ds(0, tb), :], ya_ref.at[slot], sem.at[slot]).wait()
    pltpu.make_async_copy(y_hbm.at[pl.ds(0, tb), :], yb_ref.at[slot], sem.at[slot]).wait()
    rw = rw_ref[...]
    f = rw[:, 0:1] * ya_ref[slot] + rw[:, 1:2] * yb_ref[slot]
    x = x_ref[...] + gate_ref[...] * f
    o_ref[...] = _rms_mod(x, g_ref[...], sh_ref[...], sc_ref[...])


def _combine(dest, ys, x, rw, gate, g, sh, sc, seq):
    t, d = x.shape
    tb = COMBINE_TILE
    per_b = seq // tb
    n = t // tb
    vec = lambda: pl.BlockSpec((1, d), lambda i: (0, 0))
    bvec = lambda: pl.BlockSpec((None, 1, d), lambda i: (i // per_b, 0, 0))
    return pl.pallas_call(
        functools.partial(_combine_kernel, tb=tb),
        grid=(n,),
        in_specs=[
            pl.BlockSpec((None, 2, tb), lambda i: (0, 0, 0), memory_space=pltpu.SMEM),
            pl.BlockSpec((None, 2, tb), lambda i: (jnp.minimum(i + 1, n - 1), 0, 0), memory_space=pltpu.SMEM),
            pl.BlockSpec(memory_space=pl.ANY),
            pl.BlockSpec((tb, d), lambda i: (i, 0)),
            pl.BlockSpec((tb, LANES), lambda i: (i, 0)),
            bvec(), vec(), bvec(), bvec(),
        ],
        out_specs=pl.BlockSpec((tb, d), lambda i: (i, 0)),
        out_shape=jax.ShapeDtypeStruct((t, d), F32),
        scratch_shapes=[pltpu.VMEM((2, tb, d), F32), pltpu.VMEM((2, tb, d), F32), pltpu.SemaphoreType.DMA((2,))],
        compiler_params=_params(),
        name="l1_moe_combine",
    )(dest, dest, ys, x, rw, gate, g.reshape(1, d), sh, sc)


def _moe_plan(idx, t):
    tm = MOE_TILE
    n_tiles = (2 * t) // tm + N_EXPERTS
    e_flat = idx.reshape(-1)
    experts = jnp.arange(N_EXPERTS, dtype=jnp.int32)
    onehot = (e_flat[:, None] == experts[None, :]).astype(jnp.int32)
    incl = jnp.cumsum(onehot, axis=0)
    counts = incl[-1]
    pos = jnp.sum((incl - onehot) * onehot, axis=1)
    tiles_per_e = (counts + tm - 1) // tm
    tile_end = jnp.cumsum(tiles_per_e)
    row_start = (tile_end - tiles_per_e) * tm
    dest = (row_start[e_flat] + pos).astype(jnp.int32)
    n_used = tile_end[-1:]
    tile_ids = jnp.arange(n_tiles, dtype=jnp.int32)
    tile_expert = jnp.minimum(jnp.sum((tile_ids[:, None] >= tile_end[None, :]).astype(jnp.int32), axis=1),
                              N_EXPERTS - 1)
    last_e = jnp.max(jnp.where(counts > 0, experts, 0))
    tile_expert = jnp.where(tile_ids < n_used[0], tile_expert, last_e).astype(jnp.int32)
    last_tile = jnp.maximum(tile_end - 1, 0)
    unused = jnp.minimum(n_used[0] + experts, n_tiles - 1)
    clear_tiles = jnp.concatenate([last_tile, unused]).astype(jnp.int32)
    return dest, tile_expert, n_used.astype(jnp.int32), clear_tiles, n_tiles


def _by_tile(dest, t, tb):
    return dest.reshape(t // tb, tb, 2).transpose(0, 2, 1)


def kernel(x, c, l0_ada_w, l0_ada_b, l0_norm1_g, l0_norm2_g, l0_ret_w_in, l0_ret_decay_logit, l0_ret_gn_g, l0_ret_w_out, l0_ffn_w_gate, l0_ffn_w_up, l0_ffn_w_down, l1_ada_w, l1_ada_b, l1_norm1_g, l1_norm2_g, l1_attn_w_qkv, l1_attn_q_norm_g, l1_attn_k_norm_g, l1_attn_w_out, l1_moe_w_router, l1_moe_b_router, l1_moe_w_gate, l1_moe_w_up, l1_moe_w_down, final_ada_w, final_ada_b, final_norm_g):
    bsz, seq, d = x.shape
    t = bsz * seq
    xf = x.reshape(t, d)

    c8 = jnp.zeros((8, d), F32).at[:bsz].set(c)

    def mods(w, b, n):
        m = _adaln(c8, w, b)[:bsz]
        return [m[:, i * d:(i + 1) * d].reshape(bsz, 1, d) for i in range(n)]

    sh1, sc1, g1, sh2, sc2, g2 = mods(l0_ada_w, l0_ada_b, 6)
    sh3, sc3, g3, sh4, sc4, g4 = mods(l1_ada_w, l1_ada_b, 6)
    fsh, fsc = mods(final_ada_w, final_ada_b, 2)

    bf = lambda w: w.astype(BF16)

    proj = _inproj0(xf, l0_norm1_g, sh1, sc1, bf(l0_ret_w_in), seq)
    log_gamma = jax.nn.log_sigmoid(l0_ret_decay_logit.astype(F32))
    u = _retention(proj, log_gamma, l0_ret_gn_g, bsz, seq)
    x1, h2 = _outproj(u, bf(l0_ret_w_out), xf, g1, l0_norm2_g, sh2, sc2, seq)
    x2, h3 = _ffn0(h2, bf(l0_ffn_w_gate), bf(l0_ffn_w_up), bf(l0_ffn_w_down), x1, g2, l1_norm1_g, sh3, sc3, seq)

    qkv, norm2 = _qkv1(h3, bf(l1_attn_w_qkv), l1_attn_q_norm_g, l1_attn_k_norm_g, seq)
    o = _attention(qkv, norm2, bsz, seq)
    wr = jnp.zeros((d, LANES), F32).at[:, :N_EXPERTS].set(l1_moe_w_router)
    wr_hi = wr.astype(BF16)
    wr_lo = (wr - wr_hi.astype(F32)).astype(BF16)
    br = jnp.full((1, LANES), -jnp.inf, F32).at[0, :N_EXPERTS].set(l1_moe_b_router.astype(F32))
    x3, h4, idx, rw = _outproj(o, bf(l1_attn_w_out), x2, g3, l1_norm2_g, sh4, sc4, seq,
                               router=(jnp.concatenate([wr_hi, wr_lo], axis=1), br), h_dtype=F32)
    dest, tile_expert, n_used, clear_tiles, n_tiles = _moe_plan(idx[:, :2], t)
    xs = _dispatch(h4, _by_tile(dest, t, DISPATCH_TILE), clear_tiles, n_tiles)
    ys = _moe_ffn(xs, tile_expert, n_used, bf(l1_moe_w_gate), bf(l1_moe_w_up), bf(l1_moe_w_down))
    out = _combine(_by_tile(dest, t, COMBINE_TILE), ys, x3, rw, g4, final_norm_g, fsh, fsc, seq)
    return out.reshape(bsz, seq, d)
```
